```python
import math
import jax, jax.numpy as jnp
from jax import lax
import numpy as np

D_MODEL = 1024
BATCH = 4
SEQ = 4096
DEPTH = 4
DEC_BATCH = 128
DEC_SEQ = 8
PAST_LEN = 2048
PAGE_SIZE = 128

N_MIXERS = 3
N_HEADS = 8
HEAD_DIM = D_MODEL // N_HEADS
MLSTM_QK_DIM = HEAD_DIM // 2
MLSTM_V_DIM = HEAD_DIM
MLSTM_CHUNK = 64
FOX_QBLOCK = 128
MOBA_BLOCK = 256
MOBA_TOPK = 3
MOBA_QBLOCK = 32
ROPE_THETA = 10000.0
D_FF = 2816
CONV_W = 3
RMS_EPS = 1e-6
N_A = len(range(0, DEPTH, N_MIXERS))
N_B = len(range(1, DEPTH, N_MIXERS))
N_C = len(range(2, DEPTH, N_MIXERS))

kernel_name = 'hybrid_mlstm_fox_moba_convffn_step'


def _rmsnorm(x, g):
    xf = x.astype(jnp.float32)
    y = xf * lax.rsqrt(jnp.mean(xf * xf, axis=-1, keepdims=True) + RMS_EPS)
    return (y * g.astype(jnp.float32)).astype(x.dtype)


def _rope(x, pos):
    half = x.shape[-1] // 2
    inv = ROPE_THETA ** (-jnp.arange(half, dtype=jnp.float32) / half)
    ang = pos.astype(jnp.float32)[:, None] * inv[None, :]
    cos = jnp.cos(ang)[None, :, None, :]
    sin = jnp.sin(ang)[None, :, None, :]
    xf = x.astype(jnp.float32)
    x1, x2 = xf[..., :half], xf[..., half:]
    return jnp.concatenate([x1 * cos - x2 * sin, x2 * cos + x1 * sin], axis=-1).astype(x.dtype)


def _gather_pages(pool, page_table):
    g = pool[page_table]
    return g.reshape(g.shape[0], g.shape[1] * g.shape[2], *g.shape[3:])


def _mlstm_chunkwise(q, k, v, ig, lf, c0, n0, m0, chunk):
    B, T, H, _ = q.shape
    nc = T // chunk
    causal = jnp.tril(jnp.ones((chunk, chunk), dtype=bool))

    def to_chunks(a):
        return a.astype(jnp.float32).reshape(B, nc, chunk, *a.shape[2:]).swapaxes(0, 1)

    def step(carry, xs):
        C, n, m = carry
        qc, kc, vc, ic, fc = xs
        bt = jnp.cumsum(fc, axis=1).transpose(0, 2, 1)
        it = ic.transpose(0, 2, 1)
        D = bt[:, :, :, None] - bt[:, :, None, :] + it[:, :, None, :]
        D = jnp.where(causal[None, None], D, -jnp.inf)
        inter = bt + m[:, :, None]
        mt = jnp.maximum(jnp.max(D, axis=-1), inter)
        w_intra = jnp.exp(D - mt[..., None])
        w_inter = jnp.exp(inter - mt)
        s = jnp.einsum('blhd,bshd->bhls', qc, kc) * w_intra
        num = (jnp.einsum('bhls,bshe->blhe', s, vc)
               + jnp.einsum('blhd,bhde->blhe', qc, C) * w_inter.transpose(0, 2, 1)[..., None])
        den = jnp.sum(s, axis=-1) + jnp.einsum('blhd,bhd->bhl', qc, n) * w_inter
        den = jnp.maximum(jnp.abs(den), jnp.exp(-mt))
        h = num / den.transpose(0, 2, 1)[..., None]
        bL = bt[:, :, -1]
        g = bL[:, :, None] - bt + it
        m_new = jnp.maximum(bL + m, jnp.max(g, axis=-1))
        a_prev = jnp.exp(bL + m - m_new)
        a_row = jnp.exp(g - m_new[..., None])
        C_new = a_prev[..., None, None] * C + jnp.einsum('bhs,bshd,bshe->bhde', a_row, kc, vc)
        n_new = a_prev[..., None] * n + jnp.einsum('bhs,bshd->bhd', a_row, kc)
        return (C_new, n_new, m_new), h

    carry0 = (c0.astype(jnp.float32), n0.astype(jnp.float32), m0.astype(jnp.float32))
    (C, n, m), hs = lax.scan(step, carry0, (to_chunks(q), to_chunks(k), to_chunks(v), to_chunks(ig), to_chunks(lf)))
    h = hs.swapaxes(0, 1).reshape(B, T, H, v.shape[-1])
    return h, C.astype(c0.dtype), n.astype(n0.dtype), m.astype(m0.dtype)


def _mlstm_mixer(xn, w_in, b_gates, norm_g, w_out, c0, n0, m0):
    B, T, _ = xn.shape
    H, dk, dv = N_HEADS, MLSTM_QK_DIM, MLSTM_V_DIM
    proj = xn @ w_in
    offs = [H * dk, 2 * H * dk, 2 * H * dk + H * dv, 2 * H * dk + 2 * H * dv, 2 * H * dk + 2 * H * dv + H]
    q, k, v, o, ig, fg = jnp.split(proj, offs, axis=-1)
    q = q.reshape(B, T, H, dk) * (dk ** -0.5)
    k = k.reshape(B, T, H, dk)
    v = v.reshape(B, T, H, dv)
    ig = (ig + b_gates[:H]).astype(jnp.float32)
    lf = jax.nn.log_sigmoid((fg + b_gates[H:]).astype(jnp.float32))
    chunk = MLSTM_CHUNK if T % MLSTM_CHUNK == 0 else T
    h, c, n, m = _mlstm_chunkwise(q, k, v, ig, lf, c0, n0, m0, chunk)
    h = h * lax.rsqrt(jnp.mean(h * h, axis=-1, keepdims=True) + RMS_EPS) * norm_g.astype(jnp.float32).reshape(H, dv)
    h = h * jax.nn.sigmoid(o.astype(jnp.float32)).reshape(B, T, H, dv)
    y = h.astype(xn.dtype).reshape(B, T, H * dv) @ w_out
    return y, c, n, m


def _fox_project(xn, w_in, b_f):
    B, T, _ = xn.shape
    hd = N_HEADS * HEAD_DIM
    q, k, v, f = jnp.split(xn @ w_in, [hd, 2 * hd, 3 * hd], axis=-1)
    shp = (B, T, N_HEADS, HEAD_DIM)
    logf = jax.nn.log_sigmoid((f + b_f).astype(jnp.float32))
    return q.reshape(shp), k.reshape(shp), v.reshape(shp), logf


def _fox_logits(q, k, cq, ck):
    s = jnp.einsum('bqhd,bkhd->bhqk', q, k, preferred_element_type=jnp.float32) * (q.shape[-1] ** -0.5)
    return s + cq.transpose(0, 2, 1)[:, :, :, None] - ck.transpose(0, 2, 1)[:, :, None, :]


def _fox_prompt_attend(q, k, v, logf):
    B, S, H, Dh = q.shape
    c = jnp.cumsum(logf, axis=1)
    nq = S // FOX_QBLOCK
    kpos = jnp.arange(S, dtype=jnp.int32)

    def block(args):
        qc, cc, pc = args
        s = jnp.where(kpos[None, :] <= pc[:, None], _fox_logits(qc, k, cc, c), -jnp.inf)
        p = jax.nn.softmax(s, axis=-1)
        return jnp.einsum('bhqk,bkhd->bqhd', p.astype(v.dtype), v)

    o = lax.map(block, (q.reshape(B, nq, FOX_QBLOCK, H, Dh).swapaxes(0, 1),
                        c.reshape(B, nq, FOX_QBLOCK, H).swapaxes(0, 1),
                        kpos.reshape(nq, FOX_QBLOCK)))
    return o.swapaxes(0, 1).reshape(B, S, H, Dh)


def _fox_sample_attend(q, k, v, logf, k_past, v_past, logf_past):
    P, T = k_past.shape[1], q.shape[1]
    c = jnp.cumsum(jnp.concatenate([logf_past.astype(jnp.float32), logf], axis=1), axis=1)
    c_past, c_new = c[:, :P], c[:, P:]
    s_past = _fox_logits(q, k_past, c_new, c_past)
    tpos = jnp.arange(T)
    s_new = jnp.where(tpos[None, :] <= tpos[:, None], _fox_logits(q, k, c_new, c_new), -jnp.inf)
    p = jax.nn.softmax(jnp.concatenate([s_past, s_new], axis=-1), axis=-1).astype(v.dtype)
    return (jnp.einsum('bhqk,bkhd->bqhd', p[..., :P], v_past)
            + jnp.einsum('bhqk,bkhd->bqhd', p[..., P:], v))


def _moba_project(xn, w_in, pos):
    B, T, _ = xn.shape
    shp = (B, T, N_HEADS, HEAD_DIM)
    q, k, v = jnp.split(xn @ w_in, 3, axis=-1)
    return _rope(q.reshape(shp), pos), _rope(k.reshape(shp), pos), v.reshape(shp)


def _pad_to_blocks(a):
    L = a.shape[1]
    Lp = -(-L // MOBA_BLOCK) * MOBA_BLOCK
    return jnp.pad(a, ((0, 0), (0, Lp - L), (0, 0), (0, 0)))


def _moba_attend(q, qpos, k_all, v_all, qb):
    B, T, H, Dh = q.shape
    NB = k_all.shape[1] // MOBA_BLOCK
    kb = k_all.reshape(B, NB, MOBA_BLOCK, H, Dh).transpose(0, 1, 3, 2, 4)
    vb = v_all.reshape(B, NB, MOBA_BLOCK, H, Dh).transpose(0, 1, 3, 2, 4)
    kmean = jnp.mean(kb.astype(jnp.float32), axis=3)
    n_sel = min(MOBA_TOPK, NB)
    scale = Dh ** -0.5
    b_ix = jnp.arange(B)[:, None, None]
    h_ix = jnp.arange(H)[None, None, :]
    blk_ix = jnp.arange(NB)
    off = jnp.arange(MOBA_BLOCK)

    def block(args):
        qc, pc = args
        own = pc // MOBA_BLOCK
        gate = jnp.einsum('bqhd,bnhd->bqhn', qc.astype(jnp.float32), kmean)
        gate = jnp.where((blk_ix[None, :] < own[:, None])[None, :, None, :], gate, -jnp.inf)
        gval, sel = lax.top_k(gate, n_sel)
        logits = []
        for r in range(n_sel):
            kg = kb[b_ix, sel[..., r], h_ix]
            s = jnp.einsum('bqhd,bqhkd->bqhk', qc, kg, preferred_element_type=jnp.float32) * scale
            logits.append(jnp.where(jnp.isfinite(gval[..., r])[..., None], s, -jnp.inf))
        k_own = lax.dynamic_index_in_dim(kb, own[0], axis=1, keepdims=False)
        v_own = lax.dynamic_index_in_dim(vb, own[0], axis=1, keepdims=False)
        s_own = jnp.einsum('bqhd,bhkd->bqhk', qc, k_own, preferred_element_type=jnp.float32) * scale
        own_ok = (own[:, None] * MOBA_BLOCK + off[None, :]) <= pc[:, None]
        logits.append(jnp.where(own_ok[None, :, None, :], s_own, -jnp.inf))
        p = jax.nn.softmax(jnp.concatenate(logits, axis=-1), axis=-1).astype(vb.dtype)
        out = jnp.einsum('bqhk,bhkd->bqhd', p[..., n_sel * MOBA_BLOCK:], v_own)
        for r in range(n_sel):
            vg = vb[b_ix, sel[..., r], h_ix]
            out = out + jnp.einsum('bqhk,bqhkd->bqhd', p[..., r * MOBA_BLOCK:(r + 1) * MOBA_BLOCK], vg)
        return out.astype(q.dtype)

    nq = T // qb
    o = lax.map(block, (q.reshape(B, nq, qb, H, Dh).swapaxes(0, 1), qpos.reshape(nq, qb)))
    return o.swapaxes(0, 1).reshape(B, T, H, Dh)


def _conv_ffn(xn, w_up, conv_w, conv_b, w_down, prev):
    T = xn.shape[1]
    u = xn @ w_up
    up = jnp.concatenate([prev.astype(u.dtype), u], axis=1)
    y = conv_b + conv_w[0] * up[:, 0:T]
    for j in range(1, CONV_W):
        y = y + conv_w[j] * up[:, j:j + T]
    gate, val = jnp.split(y, 2, axis=-1)
    return (jax.nn.silu(gate) * val) @ w_down, up[:, T:]


def setup_inputs(seed: int = 0) -> dict:
    key = jax.random.key(seed)
    ks = jax.random.split(key, 32)
    f32 = jnp.float32
    H, Dh, dk, dv, D = N_HEADS, HEAD_DIM, MLSTM_QK_DIM, MLSTM_V_DIM, D_MODEL
    n_pages = PAST_LEN // PAGE_SIZE
    n_pool = (DEC_BATCH * n_pages * 5) // 4

    def nrm(i, shape, scale):
        return jax.random.normal(ks[i], shape, f32) * scale

    mlstm_cols = 2 * H * dk + 2 * H * dv + 2 * H
    page_table = jax.random.permutation(ks[11], n_pool)[:DEC_BATCH * n_pages].reshape(DEC_BATCH, n_pages).astype(jnp.int32)
    return {
        'x_prompt': nrm(0, (BATCH, SEQ, D), 1.0),
        'x_sample': nrm(1, (DEC_BATCH, DEC_SEQ, D), 1.0),
        'cache_fox_k': nrm(2, (N_B, n_pool, PAGE_SIZE, H, Dh), 1.0),
        'cache_fox_v': nrm(3, (N_B, n_pool, PAGE_SIZE, H, Dh), 1.0),
        'cache_fox_logf': jax.nn.log_sigmoid(3.0 + nrm(4, (N_B, n_pool, PAGE_SIZE, H), 1.0)),
        'cache_moba_k': nrm(5, (N_C, n_pool, PAGE_SIZE, H, Dh), 1.0),
        'cache_moba_v': nrm(6, (N_C, n_pool, PAGE_SIZE, H, Dh), 1.0),
        'state_mlstm_c': nrm(7, (N_A, DEC_BATCH, H, dk, dv), 0.5),
        'state_mlstm_n': nrm(8, (N_A, DEC_BATCH, H, dk), 0.5),
        'state_mlstm_m': nrm(9, (N_A, DEC_BATCH, H), 1.0),
        'state_ffn_conv': nrm(10, (DEPTH, DEC_BATCH, CONV_W - 1, 2 * D_FF), 1.0),
        'page_table': page_table,
        'norm_mix_g': 1.0 + nrm(12, (DEPTH, D), 0.05),
        'norm_ffn_g': 1.0 + nrm(13, (DEPTH, D), 0.05),
        'norm_final_g': 1.0 + nrm(14, (D,), 0.05),
        'mlstm_w_in': nrm(15, (N_A, D, mlstm_cols), D ** -0.5),
        'mlstm_b_gates': jnp.concatenate([nrm(16, (N_A, H), 0.1),
                                          3.0 + 3.0 * jax.random.uniform(ks[17], (N_A, H), f32)], axis=-1),
        'mlstm_norm_g': 1.0 + nrm(18, (N_A, H * dv), 0.05),
        'mlstm_w_out': nrm(19, (N_A, H * dv, D), (H * dv) ** -0.5),
        'fox_w_in': nrm(20, (N_B, D, 3 * H * Dh + H), D ** -0.5),
        'fox_b_f': 1.0 + 4.0 * jax.random.uniform(ks[21], (N_B, H), f32),
        'fox_w_out': nrm(22, (N_B, H * Dh, D), (H * Dh) ** -0.5),
        'moba_w_in': nrm(23, (N_C, D, 3 * H * Dh), D ** -0.5),
        'moba_w_out': nrm(24, (N_C, H * Dh, D), (H * Dh) ** -0.5),
        'ffn_w_up': nrm(25, (DEPTH, D, 2 * D_FF), D ** -0.5),
        'ffn_conv_w': nrm(26, (DEPTH, CONV_W, 2 * D_FF), 0.5),
        'ffn_conv_b': nrm(27, (DEPTH, 2 * D_FF), 0.02),
        'ffn_w_down': nrm(28, (DEPTH, D_FF, D), D_FF ** -0.5),
    }


def reference(x_prompt, x_sample, cache_fox_k, cache_fox_v, cache_fox_logf, cache_moba_k, cache_moba_v,
              state_mlstm_c, state_mlstm_n, state_mlstm_m, state_ffn_conv, page_table,
              norm_mix_g, norm_ffn_g, norm_final_g,
              mlstm_w_in, mlstm_b_gates, mlstm_norm_g, mlstm_w_out,
              fox_w_in, fox_b_f, fox_w_out, moba_w_in, moba_w_out,
              ffn_w_up, ffn_conv_w, ffn_conv_b, ffn_w_down):
    B, S, D = x_prompt.shape
    Bd, T, _ = x_sample.shape
    H, dk, dv = N_HEADS, MLSTM_QK_DIM, MLSTM_V_DIM
    past = page_table.shape[1] * cache_fox_k.shape[2]
    pos_p = jnp.arange(S, dtype=jnp.int32)
    pos_s = past + jnp.arange(T, dtype=jnp.int32)
    xp, xs = x_prompt, x_sample
    fkp, fks, fvp, fvs, flp, fls = [], [], [], [], [], []
    mkp, mks, mvp, mvs = [], [], [], []
    acp, acs, anp, ans, amp, ams = [], [], [], [], [], []
    cvp, cvs = [], []
    for i in range(DEPTH):
        kind, slot = i % N_MIXERS, i // N_MIXERS
        hp = _rmsnorm(xp, norm_mix_g[i])
        hs = _rmsnorm(xs, norm_mix_g[i])
        if kind == 0:
            yp, c_, n_, m_ = _mlstm_mixer(hp, mlstm_w_in[slot], mlstm_b_gates[slot], mlstm_norm_g[slot], mlstm_w_out[slot],
                                         jnp.zeros((B, H, dk, dv), xp.dtype), jnp.zeros((B, H, dk), xp.dtype),
                                         jnp.zeros((B, H), xp.dtype))
            acp.append(c_); anp.append(n_); amp.append(m_)
            ys, c_, n_, m_ = _mlstm_mixer(hs, mlstm_w_in[slot], mlstm_b_gates[slot], mlstm_norm_g[slot], mlstm_w_out[slot],
                                         state_mlstm_c[slot], state_mlstm_n[slot], state_mlstm_m[slot])
            acs.append(c_); ans.append(n_); ams.append(m_)
        elif kind == 1:
            q, k, v, lf = _fox_project(hp, fox_w_in[slot], fox_b_f[slot])
            yp = _fox_prompt_attend(q, k, v, lf).reshape(B, S, D) @ fox_w_out[slot]
            fkp.append(k); fvp.append(v); flp.append(lf)
            q, k, v, lf = _fox_project(hs, fox_w_in[slot], fox_b_f[slot])
            o = _fox_sample_attend(q, k, v, lf,
                                   _gather_pages(cache_fox_k[slot], page_table),
                                   _gather_pages(cache_fox_v[slot], page_table),
                                   _gather_pages(cache_fox_logf[slot], page_table))
            ys = o.reshape(Bd, T, D) @ fox_w_out[slot]
            fks.append(k); fvs.append(v); fls.append(lf)
        else:
            q, k, v = _moba_project(hp, moba_w_in[slot], pos_p)
            o = _moba_attend(q, pos_p, _pad_to_blocks(k), _pad_to_blocks(v), MOBA_QBLOCK)
            yp = o.reshape(B, S, D) @ moba_w_out[slot]
            mkp.append(k); mvp.append(v)
            q, k, v = _moba_project(hs, moba_w_in[slot], pos_s)
            k_all = jnp.concatenate([_gather_pages(cache_moba_k[slot], page_table).astype(k.dtype), k], axis=1)
            v_all = jnp.concatenate([_gather_pages(cache_moba_v[slot], page_table).astype(v.dtype), v], axis=1)
            o = _moba_attend(q, pos_s, _pad_to_blocks(k_all), _pad_to_blocks(v_all), 1)
            ys = o.reshape(Bd, T, D) @ moba_w_out[slot]
            mks.append(k); mvs.append(v)
        xp = xp + yp
        xs = xs + ys
        fp, st_p = _conv_ffn(_rmsnorm(xp, norm_ffn_g[i]), ffn_w_up[i], ffn_conv_w[i], ffn_conv_b[i], ffn_w_down[i],
                             jnp.zeros((B, CONV_W - 1, 2 * D_FF), xp.dtype))
        fs, st_s = _conv_ffn(_rmsnorm(xs, norm_ffn_g[i]), ffn_w_up[i], ffn_conv_w[i], ffn_conv_b[i], ffn_w_down[i],
                             state_ffn_conv[i])
        xp = xp + fp
        xs = xs + fs
        cvp.append(st_p); cvs.append(st_s)
    y_prompt = _rmsnorm(xp, norm_final_g)
    y_sample = _rmsnorm(xs, norm_final_g)
    return (y_prompt, y_sample,
            jnp.stack(fkp), jnp.stack(fks), jnp.stack(fvp), jnp.stack(fvs), jnp.stack(flp), jnp.stack(fls),
            jnp.stack(mkp), jnp.stack(mks), jnp.stack(mvp), jnp.stack(mvs),
            jnp.stack(acp), jnp.stack(acs), jnp.stack(anp), jnp.stack(ans), jnp.stack(amp), jnp.stack(ams),
            jnp.stack(cvp), jnp.stack(cvs))
```

```python
import functools

import jax
import jax.numpy as jnp
from jax import lax
from jax.experimental import pallas as pl
from jax.experimental.pallas import tpu as pltpu

F32 = jnp.float32
BF16 = jnp.bfloat16
HI = lax.Precision.HIGHEST

D_MODEL = 1024
N_HEADS = 8
HEAD_DIM = 128
MLSTM_QK = 64
D_FF = 2816
CONV_W = 3
N_MIXERS = 3
MOBA_BLOCK = 256
MOBA_TOPK = 3
ROPE_THETA = 10000.0
RMS_EPS = 1e-6
NEG_INF = float("-inf")
M_FLOOR = -1e30
LANES = 128
SUBLANES = 8
VMEM_LIMIT = 56 * 1024 * 1024

NT_DIMS = (((1,), (1,)), ((), ()))
TN_DIMS = (((0,), (0,)), ((), ()))


def _params(*sem):
    return pltpu.CompilerParams(dimension_semantics=sem, vmem_limit_bytes=VMEM_LIMIT)


def _resident(shape):
    nd = len(shape)
    return pl.BlockSpec(shape, lambda *_: (0,) * nd, pipeline_mode=pl.Buffered(1))


def _rms(x, g):
    return x * lax.rsqrt(jnp.mean(x * x, axis=-1, keepdims=True) + RMS_EPS) * g


def _iota(shape, axis):
    return lax.broadcasted_iota(jnp.int32, shape, axis)


def _eye_rows(n):
    return (_iota((n, LANES), 0) == _iota((n, LANES), 1)).astype(F32)


def _nt(a, b, precision=None):
    return lax.dot_general(a, b, NT_DIMS, precision=precision, preferred_element_type=F32)


def _norm_proj_kernel(x_ref, g_ref, w_ref, *o_refs, widths):
    xn = _rms(x_ref[...], g_ref[...]).astype(BF16)
    off = 0
    for o_ref, wd in zip(o_refs, widths):
        for c in range(0, wd, 512):
            cw = min(512, wd - c)
            o_ref[:, c:c + cw] = jnp.dot(xn, w_ref[:, off + c:off + c + cw],
                                         preferred_element_type=F32)
        off += wd


def _norm_proj(x, g, w, widths, tm=256):
    n = x.shape[0]
    tm = min(tm, n)
    assert sum(widths) == w.shape[1] and n % tm == 0
    return pl.pallas_call(
        functools.partial(_norm_proj_kernel, widths=tuple(widths)),
        grid=(n // tm,),
        in_specs=[pl.BlockSpec((tm, D_MODEL), lambda i: (i, 0)),
                  _resident((1, D_MODEL)),
                  _resident(w.shape)],
        out_specs=[pl.BlockSpec((tm, wd), lambda i: (i, 0)) for wd in widths],
        out_shape=[jax.ShapeDtypeStruct((n, wd), F32) for wd in widths],
        compiler_params=_params("parallel"),
        name="norm_proj",
    )(x, g.reshape(1, D_MODEL), w)


def _proj_res_kernel(h_ref, w_ref, x_ref, o_ref):
    o_ref[...] = x_ref[...] + jnp.dot(h_ref[...].astype(BF16), w_ref[...],
                                      preferred_element_type=F32)


def _proj_res(h, w, x, tm=512):
    n = x.shape[0]
    tm = min(tm, n)
    return pl.pallas_call(
        _proj_res_kernel,
        grid=(n // tm,),
        in_specs=[pl.BlockSpec((tm, h.shape[1]), lambda i: (i, 0)),
                  _resident(w.shape),
                  pl.BlockSpec((tm, D_MODEL), lambda i: (i, 0))],
        out_specs=pl.BlockSpec((tm, D_MODEL), lambda i: (i, 0)),
        out_shape=jax.ShapeDtypeStruct((n, D_MODEL), F32),
        compiler_params=_params("parallel"),
        name="proj_res",
    )(h, w, x)


def _final_norm_kernel(x_ref, g_ref, o_ref):
    o_ref[...] = _rms(x_ref[...], g_ref[...])


def _final_norm(x, g, tm=512):
    n = x.shape[0]
    tm = min(tm, n)
    return pl.pallas_call(
        _final_norm_kernel,
        grid=(n // tm,),
        in_specs=[pl.BlockSpec((tm, D_MODEL), lambda i: (i, 0)), _resident((1, D_MODEL))],
        out_specs=pl.BlockSpec((tm, D_MODEL), lambda i: (i, 0)),
        out_shape=jax.ShapeDtypeStruct((n, D_MODEL), F32),
        compiler_params=_params("parallel"),
        name="final_norm",
    )(x, g.reshape(1, D_MODEL))


FFN_CH = 256


def _ffn_kernel(*refs, tt, grouped):
    if grouped:
        x_ref, g_ref, wup_ref, cw_ref, cb_ref, wdn_ref, p2_ref, o_ref, u_ref, hbuf = refs
    else:
        x_ref, g_ref, wup_ref, cw_ref, cb_ref, wdn_ref, o_ref, tail_ref, hbuf, carry = refs

        @pl.when(pl.program_id(1) == 0)
        def _():
            carry[...] = jnp.zeros_like(carry)

    x = x_ref[...]
    xn = _rms(x, g_ref[...]).astype(BF16)
    row = _iota((tt, 1), 0)
    for c in range(D_FF // FFN_CH):
        ys = []
        for part in range(2):
            c0 = part * D_FF + c * FFN_CH
            cols = slice(c0, c0 + FFN_CH)
            u = jnp.dot(xn, wup_ref[:, cols], preferred_element_type=F32)
            r1 = pltpu.roll(u, 1, 0)
            r2 = pltpu.roll(u, 2, 0)
            if grouped:
                p2 = p2_ref[:, cols]
                p1 = pltpu.roll(p2, tt - 1, 0)
                t8 = row & (SUBLANES - 1)
                um1 = jnp.where(t8 < 1, p1, r1)
                um2 = jnp.where(t8 < 2, p2, r2)
                u_ref[:, cols] = u
            else:
                pc = carry[:, cols]
                row8 = row[:SUBLANES]
                f1 = jnp.where(row8 < 1, pltpu.roll(pc, 1, 0), r1[:SUBLANES])
                f2 = jnp.where(row8 < 2, pltpu.roll(pc, 2, 0), r2[:SUBLANES])
                um1 = jnp.concatenate([f1, r1[SUBLANES:]], axis=0)
                um2 = jnp.concatenate([f2, r2[SUBLANES:]], axis=0)
                tail = u[tt - SUBLANES:]
                carry[:, cols] = tail
                tail_ref[:, cols] = tail
            y = cb_ref[:, cols] + cw_ref[0:1, cols] * um2
            y = y + cw_ref[1:2, cols] * um1
            y = y + cw_ref[2:3, cols] * u
            ys.append(y)
        gate, val = ys
        hbuf[:, c * FFN_CH:(c + 1) * FFN_CH] = (gate * jax.nn.sigmoid(gate) * val).astype(BF16)
    o_ref[...] = x + jnp.dot(hbuf[...], wdn_ref[...], preferred_element_type=F32)


def _ffn_prompt(x, g, wup, cw, cb, wdn, nseq, tt=512):
    n = x.shape[0]
    nt = n // nseq // tt
    return pl.pallas_call(
        functools.partial(_ffn_kernel, tt=tt, grouped=False),
        grid=(nseq, nt),
        in_specs=[pl.BlockSpec((tt, D_MODEL), lambda b, t: (b * nt + t, 0)),
                  _resident((1, D_MODEL)), _resident(wup.shape), _resident(cw.shape),
                  _resident((1, 2 * D_FF)), _resident(wdn.shape)],
        out_specs=[pl.BlockSpec((tt, D_MODEL), lambda b, t: (b * nt + t, 0)),
                   pl.BlockSpec((SUBLANES, 2 * D_FF), lambda b, t: (b, 0))],
        out_shape=[jax.ShapeDtypeStruct((n, D_MODEL), F32),
                   jax.ShapeDtypeStruct((nseq * SUBLANES, 2 * D_FF), F32)],
        scratch_shapes=[pltpu.VMEM((tt, D_FF), BF16), pltpu.VMEM((SUBLANES, 2 * D_FF), F32)],
        compiler_params=_params("parallel", "arbitrary"),
        name="ffn_prompt",
    )(x, g.reshape(1, D_MODEL), wup, cw, cb.reshape(1, 2 * D_FF), wdn)


def _ffn_sample(x, g, wup, cw, cb, wdn, p2, tt=128):
    n = x.shape[0]
    tt = min(tt, n)
    return pl.pallas_call(
        functools.partial(_ffn_kernel, tt=tt, grouped=True),
        grid=(n // tt,),
        in_specs=[pl.BlockSpec((tt, D_MODEL), lambda i: (i, 0)),
                  _resident((1, D_MODEL)), _resident(wup.shape), _resident(cw.shape),
                  _resident((1, 2 * D_FF)), _resident(wdn.shape),
                  pl.BlockSpec((tt, 2 * D_FF), lambda i: (i, 0))],
        out_specs=[pl.BlockSpec((tt, D_MODEL), lambda i: (i, 0)),
                   pl.BlockSpec((tt, 2 * D_FF), lambda i: (i, 0))],
        out_shape=[jax.ShapeDtypeStruct((n, D_MODEL), F32),
                   jax.ShapeDtypeStruct((n, 2 * D_FF), F32)],
        scratch_shapes=[pltpu.VMEM((tt, D_FF), BF16)],
        compiler_params=_params("parallel"),
        name="ffn_sample",
    )(x, g.reshape(1, D_MODEL), wup, cw, cb.reshape(1, 2 * D_FF), wdn, p2)


def _mlstm_prompt_kernel(q_ref, k_ref, v_ref, o_ref, ig_ref, fg_ref, bi_ref, bf_ref, ng_ref,
                         h_ref, cout_ref, nout_ref, mout_ref, c_s, n_s, m_s, *, chunk):
    L = chunk
    j = pl.program_id(1)

    @pl.when(j == 0)
    def _():
        c_s[...] = jnp.zeros_like(c_s)
        n_s[...] = jnp.zeros_like(n_s)
        m_s[...] = jnp.zeros_like(m_s)

    ig = ig_ref[...] + bi_ref[...]
    lf = jax.nn.log_sigmoid(fg_ref[...] + bf_ref[...])
    tril = (_iota((L, L), 0) >= _iota((L, L), 1))
    bt = jnp.dot(tril.astype(F32), lf, precision=HI, preferred_element_type=F32)
    rc = ig - bt
    r_t = _nt(_eye_rows(SUBLANES), rc, HI)
    for h in range(N_HEADS):
        hs = slice(h * HEAD_DIM, (h + 1) * HEAD_DIM)
        qh, kh, vh = q_ref[:, hs], k_ref[:, hs], v_ref[:, hs]
        qb, kb, vb = qh.astype(BF16), kh.astype(BF16), vh.astype(BF16)
        bcol = bt[:, h:h + 1]
        dm = jnp.where(tril, bcol + r_t[h:h + 1, :], NEG_INF)
        m_prev = m_s[h][:, 0:1]
        inter = bcol + m_prev
        mt = jnp.maximum(jnp.max(dm, axis=-1, keepdims=True), inter)
        w_intra = jnp.exp(dm - mt)
        w_inter = jnp.exp(inter - mt)
        c_prev = c_s[h]
        n_prev = n_s[h]
        s = _nt(qb, kb) * w_intra
        num = (jnp.dot(s.astype(BF16), vb, preferred_element_type=F32)
               + jnp.dot(qb, c_prev.astype(BF16), preferred_element_type=F32) * w_inter)
        den = (jnp.sum(s, axis=-1, keepdims=True)
               + jnp.sum(qh * n_prev, axis=-1, keepdims=True) * w_inter)
        den = jnp.maximum(jnp.abs(den), jnp.exp(-mt))
        hh = num / den
        hh = hh * lax.rsqrt(jnp.mean(hh * hh, axis=-1, keepdims=True) + RMS_EPS) * ng_ref[:, hs]
        h_ref[:, hs] = hh * jax.nn.sigmoid(o_ref[:, hs])
        b_last = bcol[L - 1:L, :]
        gcol = b_last + rc[:, h:h + 1]
        m_new = jnp.maximum(b_last + m_prev, jnp.max(gcol, axis=0, keepdims=True))
        a_prev = jnp.exp(b_last + m_prev - m_new)
        ka = kh * jnp.exp(gcol - m_new)
        c_new = a_prev * c_prev + lax.dot_general(ka.astype(BF16), vb, TN_DIMS,
                                                  preferred_element_type=F32)
        n_new = a_prev * n_prev + jnp.sum(ka, axis=0, keepdims=True)
        c_s[h] = c_new
        n_s[h] = n_new
        m_s[h] = jnp.broadcast_to(m_new, (1, LANES))

        @pl.when(j == pl.num_programs(1) - 1)
        def _():
            cout_ref[0, h] = c_new[:MLSTM_QK, :]
            nout_ref[0, h:h + 1, :] = n_new
            mout_ref[0, :, h:h + 1] = m_new


def _mlstm_prompt(q, k, v, o, ig, fg, bi, bfg, ng, nseq, chunk=128):
    n = q.shape[0]
    nc = n // nseq // chunk
    wide = pl.BlockSpec((chunk, D_MODEL), lambda b, j: (b * nc + j, 0))
    narrow = pl.BlockSpec((chunk, LANES), lambda b, j: (b * nc + j, 0))
    return pl.pallas_call(
        functools.partial(_mlstm_prompt_kernel, chunk=chunk),
        grid=(nseq, nc),
        in_specs=[wide, wide, wide, wide, narrow, narrow,
                  _resident((1, LANES)), _resident((1, LANES)), _resident((1, D_MODEL))],
        out_specs=[wide,
                   pl.BlockSpec((1, N_HEADS, MLSTM_QK, HEAD_DIM), lambda b, j: (b, 0, 0, 0)),
                   pl.BlockSpec((1, N_HEADS, LANES), lambda b, j: (b, 0, 0)),
                   pl.BlockSpec((1, 1, N_HEADS), lambda b, j: (b, 0, 0))],
        out_shape=[jax.ShapeDtypeStruct((n, D_MODEL), F32),
                   jax.ShapeDtypeStruct((nseq, N_HEADS, MLSTM_QK, HEAD_DIM), F32),
                   jax.ShapeDtypeStruct((nseq, N_HEADS, LANES), F32),
                   jax.ShapeDtypeStruct((nseq, 1, N_HEADS), F32)],
        scratch_shapes=[pltpu.VMEM((N_HEADS, HEAD_DIM, HEAD_DIM), F32),
                        pltpu.VMEM((N_HEADS, 1, LANES), F32),
                        pltpu.VMEM((N_HEADS, 1, LANES), F32)],
        compiler_params=_params("parallel", "arbitrary"),
        name="mlstm_prompt",
    )(q, k, v, o, ig, fg, bi, bfg, ng)


def _cumsum_rows8(x):
    row = _iota(x.shape, 0)
    for sh in (1, 2, 4):
        x = x + jnp.where(row >= sh, pltpu.roll(x, sh, 0), 0.0)
    return x


def _stack_cols(x, n=N_HEADS):
    return jnp.concatenate([x[:, h:h + 1] for h in range(n)], axis=0)


def _stack_bcast(x, rows, n=N_HEADS):
    return jnp.concatenate([jnp.broadcast_to(x[:, h:h + 1], (rows, 1)) for h in range(n)], axis=0)


def _pad_rows(x, rows=LANES):
    return jnp.concatenate([x, jnp.zeros((rows - x.shape[0], x.shape[1]), x.dtype)], axis=0)


def _mlstm_sample_kernel(q_ref, k_ref, v_ref, o_ref, ig_ref, fg_ref, bi_ref, bf_ref, ng_ref,
                         c0_ref, n0_ref, m0_ref, h_ref, cout_ref, nout_ref, mout_ref):
    T, HT, QW = SUBLANES, N_HEADS * SUBLANES, N_HEADS * MLSTM_QK
    ig = ig_ref[...] + bi_ref[...]
    lf = jax.nn.log_sigmoid(fg_ref[...] + bf_ref[...])
    bt = _cumsum_rows8(lf)
    rc = ig - bt
    m0 = m0_ref[0]
    bcol = _stack_cols(bt)
    r_t = _nt(_eye_rows(SUBLANES), _pad_rows(rc), HI)
    rrow = jnp.concatenate([jnp.broadcast_to(r_t[h:h + 1, :], (T, LANES))
                            for h in range(N_HEADS)], axis=0)
    lane = _iota((HT, LANES), 1)
    tok = _iota((HT, LANES), 0) & (T - 1)
    dm = jnp.where(lane <= tok, bcol + rrow, NEG_INF)
    inter = bcol + _stack_bcast(m0, T)
    mt = jnp.maximum(jnp.max(dm, axis=-1, keepdims=True), inter)
    w_intra = jnp.exp(dm - mt)
    w_inter = jnp.exp(inter - mt)

    head_of_row = _iota((HT, QW), 0) // T
    head_of_lane = _iota((HT, QW), 1) // MLSTM_QK
    diag = head_of_row == head_of_lane
    q8, k8, v8 = q_ref[...], k_ref[...], v_ref[...]
    qbd = jnp.where(diag, jnp.concatenate([q8] * N_HEADS, axis=0), 0.0)
    qbd_b = qbd.astype(BF16)
    s = _nt(qbd_b, _pad_rows(k8).astype(BF16)) * w_intra
    nv = jnp.dot(s.astype(BF16), _pad_rows(v8).astype(BF16), preferred_element_type=F32)
    row_head = _iota((HT, HEAD_DIM), 0) // T
    num = jnp.zeros((HT, HEAD_DIM), F32)
    for h in range(N_HEADS):
        num = num + jnp.where(row_head == h, nv[:, h * HEAD_DIM:(h + 1) * HEAD_DIM], 0.0)
    c_prev = c0_ref[0].reshape(QW, HEAD_DIM)
    n_prev = n0_ref[0]
    num = num + jnp.dot(qbd_b, c_prev.astype(BF16), preferred_element_type=F32) * w_inter
    den = (jnp.sum(s, axis=-1, keepdims=True)
           + jnp.sum(qbd * n_prev, axis=-1, keepdims=True) * w_inter)
    den = jnp.maximum(jnp.abs(den), jnp.exp(-mt))
    hh = num / den
    ng = jnp.concatenate([jnp.broadcast_to(ng_ref[:, h * HEAD_DIM:(h + 1) * HEAD_DIM], (T, HEAD_DIM))
                          for h in range(N_HEADS)], axis=0)
    o8 = o_ref[...]
    ost = jnp.concatenate([o8[:, h * HEAD_DIM:(h + 1) * HEAD_DIM] for h in range(N_HEADS)], axis=0)
    hh = hh * lax.rsqrt(jnp.mean(hh * hh, axis=-1, keepdims=True) + RMS_EPS) * ng
    hh = hh * jax.nn.sigmoid(ost)
    h_ref[...] = jnp.concatenate([hh[h * T:(h + 1) * T, :] for h in range(N_HEADS)], axis=1)

    b_last = bt[T - 1:T, :]
    g2 = b_last + rc
    m_new = jnp.maximum(b_last + m0, jnp.max(g2, axis=0, keepdims=True))
    a_prev = jnp.exp(b_last + m0 - m_new)
    a_tok = jnp.exp(g2 - m_new)
    kabd = jnp.where(diag, jnp.concatenate([k8] * N_HEADS, axis=0) * _stack_cols(a_tok), 0.0)
    vst = jnp.concatenate([v8[:, h * HEAD_DIM:(h + 1) * HEAD_DIM] for h in range(N_HEADS)], axis=0)
    dc = lax.dot_general(kabd.astype(BF16), vst.astype(BF16), TN_DIMS, preferred_element_type=F32)
    c_new = _stack_bcast(a_prev, MLSTM_QK) * c_prev + dc
    cout_ref[0] = c_new.reshape(N_HEADS, MLSTM_QK, HEAD_DIM)
    a_lane = jnp.max(jnp.where(diag, _stack_bcast(a_prev, T), 0.0), axis=0, keepdims=True)
    nout_ref[0] = a_lane * n_prev + jnp.sum(kabd, axis=0, keepdims=True)
    mout_ref[0] = m_new[:, :N_HEADS]


def _mlstm_sample(q, k, v, o, ig, fg, bi, bfg, ng, c0, n0, m0):
    n = q.shape[0]
    nb = n // SUBLANES
    qw = N_HEADS * MLSTM_QK

    def rows(w):
        return pl.BlockSpec((SUBLANES, w), lambda b: (b, 0))

    cspec = pl.BlockSpec((1, N_HEADS, MLSTM_QK, HEAD_DIM), lambda b: (b, 0, 0, 0))
    nspec = pl.BlockSpec((1, 1, qw), lambda b: (b, 0, 0))
    mspec = pl.BlockSpec((1, 1, N_HEADS), lambda b: (b, 0, 0))
    return pl.pallas_call(
        _mlstm_sample_kernel,
        grid=(nb,),
        in_specs=[rows(qw), rows(qw), rows(D_MODEL), rows(D_MODEL), rows(LANES), rows(LANES),
                  _resident((1, LANES)), _resident((1, LANES)), _resident((1, D_MODEL)),
                  cspec, nspec, pl.BlockSpec((1, 1, LANES), lambda b: (b, 0, 0))],
        out_specs=[rows(D_MODEL), cspec, nspec, mspec],
        out_shape=[jax.ShapeDtypeStruct((n, D_MODEL), F32),
                   jax.ShapeDtypeStruct((nb, N_HEADS, MLSTM_QK, HEAD_DIM), F32),
                   jax.ShapeDtypeStruct((nb, 1, qw), F32),
                   jax.ShapeDtypeStruct((nb, 1, N_HEADS), F32)],
        compiler_params=_params("parallel"),
        name="mlstm_sample",
    )(q, k, v, o, ig, fg, bi, bfg, ng, c0, n0.reshape(nb, 1, qw), _pad_lanes(m0).reshape(nb, 1, LANES))


def _fox_gate_kernel(f_ref, bf_ref, lf_ref, ccol_ref, crow_ref, carry, *, ts):
    @pl.when(pl.program_id(1) == 0)
    def _():
        carry[...] = jnp.zeros_like(carry)

    lf = jax.nn.log_sigmoid(f_ref[...] + bf_ref[...])
    tril = (_iota((ts, ts), 0) >= _iota((ts, ts), 1)).astype(F32)
    c = jnp.dot(tril, lf, precision=HI, preferred_element_type=F32) + carry[...]
    carry[...] = c[ts - 1:ts, :]
    lf_ref[...] = lf
    ccol_ref[...] = c
    crow_ref[0] = _nt(_eye_rows(SUBLANES), c, HI)


def _fox_gates(fraw, bfg, nseq, ts=512):
    n = fraw.shape[0]
    seq = n // nseq
    nt = seq // ts
    tok = pl.BlockSpec((ts, LANES), lambda b, t: (b * nt + t, 0))
    return pl.pallas_call(
        functools.partial(_fox_gate_kernel, ts=ts),
        grid=(nseq, nt),
        in_specs=[tok, _resident((1, LANES))],
        out_specs=[tok, tok, pl.BlockSpec((1, SUBLANES, ts), lambda b, t: (b, 0, t))],
        out_shape=[jax.ShapeDtypeStruct((n, LANES), F32), jax.ShapeDtypeStruct((n, LANES), F32),
                   jax.ShapeDtypeStruct((nseq, SUBLANES, seq), F32)],
        scratch_shapes=[pltpu.VMEM((1, LANES), F32)],
        compiler_params=_params("parallel", "arbitrary"),
        name="fox_gates",
    )(fraw, bfg)


def _online_update(s, vb, m_ref, l_ref, acc_ref, hs):
    m_prev = m_ref[...]
    m_new = jnp.maximum(m_prev, jnp.max(s, axis=-1, keepdims=True))
    alpha = jnp.exp(m_prev - m_new)
    p = jnp.exp(s - m_new)
    l_ref[...] = alpha * l_ref[...] + jnp.sum(p, axis=-1, keepdims=True)
    acc_ref[:, hs] = alpha * acc_ref[:, hs] + jnp.dot(p.astype(BF16), vb, preferred_element_type=F32)
    m_ref[...] = m_new


def _fox_prompt_kernel(q_ref, k_ref, v_ref, ccol_ref, crow_ref, o_ref, m_s, l_s, acc_s, *, tq):
    i, j = pl.program_id(1), pl.program_id(2)
    scale = HEAD_DIM ** -0.5

    @pl.when(j == 0)
    def _():
        m_s[...] = jnp.full_like(m_s, M_FLOOR)
        l_s[...] = jnp.zeros_like(l_s)
        acc_s[...] = jnp.zeros_like(acc_s)

    @pl.when(j <= i)
    def _():
        visible = (_iota((tq, tq), 1) <= _iota((tq, tq), 0)) | (j < i)
        for h in range(N_HEADS):
            hs = slice(h * HEAD_DIM, (h + 1) * HEAD_DIM)
            s = _nt(q_ref[:, hs].astype(BF16), k_ref[:, hs].astype(BF16)) * scale
            s = s + ccol_ref[:, h:h + 1] - crow_ref[0, h:h + 1, :]
            s = jnp.where(visible, s, NEG_INF)
            _online_update(s, v_ref[:, hs].astype(BF16), m_s.at[h], l_s.at[h], acc_s, hs)

    @pl.when(j == i)
    def _():
        for h in range(N_HEADS):
            hs = slice(h * HEAD_DIM, (h + 1) * HEAD_DIM)
            o_ref[:, hs] = acc_s[:, hs] / l_s[h]


def _fox_prompt(q, k, v, ccol, crow, nseq, tq=512):
    n = q.shape[0]
    nq = n // nseq // tq
    qspec = pl.BlockSpec((tq, D_MODEL), lambda b, i, j: (b * nq + i, 0))
    kspec = pl.BlockSpec((tq, D_MODEL), lambda b, i, j: (b * nq + jnp.minimum(j, i), 0))
    return pl.pallas_call(
        functools.partial(_fox_prompt_kernel, tq=tq),
        grid=(nseq, nq, nq),
        in_specs=[qspec, kspec, kspec,
                  pl.BlockSpec((tq, LANES), lambda b, i, j: (b * nq + i, 0)),
                  pl.BlockSpec((1, SUBLANES, tq), lambda b, i, j: (b, 0, jnp.minimum(j, i)))],
        out_specs=qspec,
        out_shape=jax.ShapeDtypeStruct((n, D_MODEL), F32),
        scratch_shapes=[pltpu.VMEM((N_HEADS, tq, 1), F32), pltpu.VMEM((N_HEADS, tq, 1), F32),
                        pltpu.VMEM((tq, D_MODEL), F32)],
        compiler_params=_params("parallel", "parallel", "arbitrary"),
        name="fox_prompt",
    )(q, k, v, ccol, crow)


HT_ROWS = N_HEADS * SUBLANES


def _block_diag_mask():
    return (_iota((HT_ROWS, D_MODEL), 0) & (N_HEADS - 1)) == (_iota((HT_ROWS, D_MODEL), 1) // HEAD_DIM)


def _block_diag_queries(q8):
    rep = jnp.concatenate([jnp.broadcast_to(q8[t:t + 1, :], (N_HEADS, D_MODEL))
                           for t in range(SUBLANES)], axis=0)
    return jnp.where(_block_diag_mask(), rep, 0.0)


def _read_diag(x):
    x = jnp.where(_block_diag_mask(), x, 0.0)
    return jnp.sum(x.reshape(SUBLANES, N_HEADS, D_MODEL), axis=1)


def _new_token_mask():
    return _iota((HT_ROWS, LANES), 1) <= (_iota((HT_ROWS, LANES), 0) // N_HEADS)


def _fox_sample_kernel(pt_ref, q_ref, kn_ref, vn_ref, f_ref, bf_ref, kc_ref, vc_ref, lfc_ref,
                       o_ref, lf_ref, qbd_s, m_s, l_s, acc_s, carry_s):
    p = pl.program_id(1)
    scale = HEAD_DIM ** -0.5
    all_lanes = slice(0, D_MODEL)

    @pl.when(p == 0)
    def _():
        qbd_s[...] = _block_diag_queries(q_ref[...]).astype(BF16)
        m_s[...] = jnp.full_like(m_s, M_FLOOR)
        l_s[...] = jnp.zeros_like(l_s)
        acc_s[...] = jnp.zeros_like(acc_s)
        carry_s[...] = jnp.zeros_like(carry_s)

    triu = (_iota((LANES, LANES), 0) <= _iota((LANES, LANES), 1)).astype(F32)
    cpage = jnp.dot(lfc_ref[0], triu, precision=HI, preferred_element_type=F32) + carry_s[:, 0:1]
    carry_s[...] = jnp.broadcast_to(cpage[:, LANES - 1:LANES], (SUBLANES, LANES))
    s = _nt(qbd_s[...], kc_ref[0].astype(BF16)) * scale - jnp.concatenate([cpage] * SUBLANES, axis=0)
    _online_update(s, vc_ref[0].astype(BF16), m_s, l_s, acc_s, all_lanes)

    @pl.when(p == pl.num_programs(1) - 1)
    def _():
        lf = jax.nn.log_sigmoid(f_ref[...] + bf_ref[...])
        lf_ref[...] = lf
        cum_t = _nt(_eye_rows(SUBLANES), _pad_rows(_cumsum_rows8(lf)), HI)
        cnew = cum_t + carry_s[:, 0:1]
        sn = _nt(qbd_s[...], _pad_rows(kn_ref[...]).astype(BF16)) * scale
        sn = sn - jnp.concatenate([cnew] * SUBLANES, axis=0)
        sn = jnp.where(_new_token_mask(), sn, NEG_INF)
        _online_update(sn, _pad_rows(vn_ref[...]).astype(BF16), m_s, l_s, acc_s, all_lanes)
        o_ref[...] = _read_diag(acc_s[...] / l_s[...])


def _fox_sample(page_table, q, kn, vn, fraw, bfg, kc, vc, lfc):
    n = q.shape[0]
    nb, npages = page_table.shape

    def rows(w):
        return pl.BlockSpec((SUBLANES, w), lambda b, p, pt: (b, 0))

    page = pl.BlockSpec((1, LANES, D_MODEL), lambda b, p, pt: (pt[b, p], 0, 0))
    grid_spec = pltpu.PrefetchScalarGridSpec(
        num_scalar_prefetch=1,
        grid=(nb, npages),
        in_specs=[rows(D_MODEL), rows(D_MODEL), rows(D_MODEL), rows(LANES),
                  pl.BlockSpec((1, LANES), lambda b, p, pt: (0, 0)),
                  page, page,
                  pl.BlockSpec((1, SUBLANES, LANES), lambda b, p, pt: (pt[b, p], 0, 0))],
        out_specs=[rows(D_MODEL), rows(LANES)],
        scratch_shapes=[pltpu.VMEM((HT_ROWS, D_MODEL), BF16),
                        pltpu.VMEM((HT_ROWS, 1), F32), pltpu.VMEM((HT_ROWS, 1), F32),
                        pltpu.VMEM((HT_ROWS, D_MODEL), F32),
                        pltpu.VMEM((SUBLANES, LANES), F32)])
    return pl.pallas_call(
        _fox_sample_kernel,
        grid_spec=grid_spec,
        out_shape=[jax.ShapeDtypeStruct((n, D_MODEL), F32), jax.ShapeDtypeStruct((n, LANES), F32)],
        compiler_params=_params("parallel", "arbitrary"),
        name="fox_sample",
    )(page_table, q, kn, vn, fraw, bfg, kc, vc, lfc)


def _rope_kernel(q_ref, k_ref, cos_ref, sin_ref, qo_ref, ko_ref, km_ref):
    cos, sin = cos_ref[...], sin_ref[...]
    for h in range(N_HEADS):
        hs = slice(h * HEAD_DIM, (h + 1) * HEAD_DIM)
        for src, dst in ((q_ref, qo_ref), (k_ref, ko_ref)):
            x = src[:, hs]
            dst[:, hs] = x * cos + pltpu.roll(x, HEAD_DIM // 2, 1) * sin
    km_ref[0] = jnp.mean(ko_ref[...], axis=0, keepdims=True)


def _rope(q, k, cos, sin, tt=MOBA_BLOCK):
    n = q.shape[0]
    tt = min(tt, n)
    ntab = cos.shape[0] // tt
    tok = pl.BlockSpec((tt, D_MODEL), lambda i: (i, 0))
    tab = pl.BlockSpec((tt, HEAD_DIM), lambda i: (i % ntab, 0))
    return pl.pallas_call(
        _rope_kernel,
        grid=(n // tt,),
        in_specs=[tok, tok, tab, tab],
        out_specs=[tok, tok, pl.BlockSpec((1, 1, D_MODEL), lambda i: (i, 0, 0))],
        out_shape=[jax.ShapeDtypeStruct((n, D_MODEL), F32), jax.ShapeDtypeStruct((n, D_MODEL), F32),
                   jax.ShapeDtypeStruct((n // tt, 1, D_MODEL), F32)],
        compiler_params=_params("parallel"),
        name="rope",
    )(q, k, cos, sin)


def _top_blocks(gate, n_valid):
    lane = _iota(gate.shape, 1)
    g = jnp.where(lane < n_valid, gate, NEG_INF)
    sel = jnp.zeros(gate.shape, F32)
    for _ in range(MOBA_TOPK):
        mx = jnp.max(g, axis=-1, keepdims=True)
        idx = jnp.min(jnp.where(g == mx, lane, LANES), axis=-1, keepdims=True)
        hit = lane == idx
        sel = jnp.where(hit & (mx > NEG_INF), 1.0, sel)
        g = jnp.where(hit, NEG_INF, g)
    return sel


def _moba_prompt_kernel(q_ref, k_ref, v_ref, km_ref, o_ref, sel_s, m_s, l_s, acc_s):
    i, j = pl.program_id(1), pl.program_id(2)
    tq = MOBA_BLOCK
    scale = HEAD_DIM ** -0.5

    @pl.when(j == 0)
    def _():
        m_s[...] = jnp.full_like(m_s, M_FLOOR)
        l_s[...] = jnp.zeros_like(l_s)
        acc_s[...] = jnp.zeros_like(acc_s)
        for h in range(N_HEADS):
            hs = slice(h * HEAD_DIM, (h + 1) * HEAD_DIM)
            sel_s[h] = _top_blocks(_nt(q_ref[:, hs], km_ref[0, :, hs], HI), i)

    @pl.when(j <= i)
    def _():
        causal = _iota((tq, tq), 1) <= _iota((tq, tq), 0)
        lane = _iota((tq, LANES), 1)
        for h in range(N_HEADS):
            hs = slice(h * HEAD_DIM, (h + 1) * HEAD_DIM)
            picked = jnp.max(jnp.where(lane == j, sel_s[h], 0.0), axis=-1, keepdims=True) > 0.0
            visible = (causal & (j == i)) | (picked & (j < i))
            s = _nt(q_ref[:, hs].astype(BF16), k_ref[:, hs].astype(BF16)) * scale
            s = jnp.where(visible, s, NEG_INF)
            _online_update(s, v_ref[:, hs].astype(BF16), m_s.at[h], l_s.at[h], acc_s, hs)

    @pl.when(j == i)
    def _():
        for h in range(N_HEADS):
            hs = slice(h * HEAD_DIM, (h + 1) * HEAD_DIM)
            o_ref[:, hs] = acc_s[:, hs] / l_s[h]


def _moba_prompt(q, k, v, kmean, nseq):
    n = q.shape[0]
    tq = MOBA_BLOCK
    nq = n // nseq // tq
    qspec = pl.BlockSpec((tq, D_MODEL), lambda b, i, j: (b * nq + i, 0))
    kspec = pl.BlockSpec((tq, D_MODEL), lambda b, i, j: (b * nq + jnp.minimum(j, i), 0))
    return pl.pallas_call(
        _moba_prompt_kernel,
        grid=(nseq, nq, nq),
        in_specs=[qspec, kspec, kspec,
                  pl.BlockSpec((1, LANES, D_MODEL), lambda b, i, j: (b, 0, 0))],
        out_specs=qspec,
        out_shape=jax.ShapeDtypeStruct((n, D_MODEL), F32),
        scratch_shapes=[pltpu.VMEM((N_HEADS, tq, LANES), F32),
                        pltpu.VMEM((N_HEADS, tq, 1), F32), pltpu.VMEM((N_HEADS, tq, 1), F32),
                        pltpu.VMEM((tq, D_MODEL), F32)],
        compiler_params=_params("parallel", "parallel", "arbitrary"),
        name="moba_prompt",
    )(q, k, v, kmean)


def _moba_sample_kernel(pt_ref, q_ref, kn_ref, vn_ref, kc_ref, vc_ref, o_ref,
                        qf_s, qb_s, m_all, l_all, acc_all, ksum_s, *, pages_per_block):
    p = pl.program_id(1)
    npages = pl.num_programs(1)
    scale = HEAD_DIM ** -0.5

    @pl.when(p == 0)
    def _():
        qbd = _block_diag_queries(q_ref[...])
        qf_s[...] = qbd
        qb_s[...] = qbd.astype(BF16)
        ksum_s[...] = jnp.zeros_like(ksum_s)

    kpage = kc_ref[0]
    blk = p // pages_per_block
    ksum_s[pl.ds(blk, 1), :] = ksum_s[pl.ds(blk, 1), :] + jnp.sum(kpage, axis=0, keepdims=True)
    s = _nt(qb_s[...], kpage.astype(BF16)) * scale
    mp = jnp.max(s, axis=-1, keepdims=True)
    e = jnp.exp(s - mp)
    m_all[p] = mp
    l_all[p] = jnp.sum(e, axis=-1, keepdims=True)
    acc_all[p] = jnp.dot(e.astype(BF16), vc_ref[0].astype(BF16), preferred_element_type=F32)

    @pl.when(p == npages - 1)
    def _():
        n_blocks = npages // pages_per_block
        kmean = ksum_s[...] * (1.0 / MOBA_BLOCK)
        sel = _top_blocks(_nt(qf_s[...], kmean, HI), n_blocks)
        so = _nt(qb_s[...], _pad_rows(kn_ref[...]).astype(BF16)) * scale
        so = jnp.where(_new_token_mask(), so, NEG_INF)
        m_own = jnp.max(so, axis=-1, keepdims=True)
        e_own = jnp.exp(so - m_own)
        l_own = jnp.sum(e_own, axis=-1, keepdims=True)
        acc_own = jnp.dot(e_own.astype(BF16), _pad_rows(vn_ref[...]).astype(BF16),
                          preferred_element_type=F32)
        picks = [sel[:, (pg // pages_per_block):(pg // pages_per_block) + 1] > 0.0
                 for pg in range(acc_all.shape[0])]
        m_tot = m_own
        for pg, pick in enumerate(picks):
            m_tot = jnp.maximum(m_tot, jnp.where(pick, m_all[pg], NEG_INF))
        w_own = jnp.exp(m_own - m_tot)
        acc = w_own * acc_own
        l_tot = w_own * l_own
        for pg, pick in enumerate(picks):
            w = jnp.where(pick, jnp.exp(m_all[pg] - m_tot), 0.0)
            acc = acc + w * acc_all[pg]
            l_tot = l_tot + w * l_all[pg]
        o_ref[...] = _read_diag(acc / l_tot)


def _moba_sample(page_table, q, kn, vn, kc, vc):
    n = q.shape[0]
    nb, npages = page_table.shape
    page_rows = kc.shape[1]

    def rows(w):
        return pl.BlockSpec((SUBLANES, w), lambda b, p, pt: (b, 0))

    page = pl.BlockSpec((1, page_rows, D_MODEL), lambda b, p, pt: (pt[b, p], 0, 0))
    grid_spec = pltpu.PrefetchScalarGridSpec(
        num_scalar_prefetch=1,
        grid=(nb, npages),
        in_specs=[rows(D_MODEL), rows(D_MODEL), rows(D_MODEL), page, page],
        out_specs=rows(D_MODEL),
        scratch_shapes=[pltpu.VMEM((HT_ROWS, D_MODEL), F32), pltpu.VMEM((HT_ROWS, D_MODEL), BF16),
                        pltpu.VMEM((npages, HT_ROWS, 1), F32), pltpu.VMEM((npages, HT_ROWS, 1), F32),
                        pltpu.VMEM((npages, HT_ROWS, D_MODEL), F32),
                        pltpu.VMEM((LANES, D_MODEL), F32)])
    return pl.pallas_call(
        functools.partial(_moba_sample_kernel, pages_per_block=MOBA_BLOCK // page_rows),
        grid_spec=grid_spec,
        out_shape=jax.ShapeDtypeStruct((n, D_MODEL), F32),
        compiler_params=_params("parallel", "arbitrary"),
        name="moba_sample",
    )(page_table, q, kn, vn, kc, vc)


def _pad_lanes(a, width=LANES):
    return jnp.pad(a, [(0, 0)] * (a.ndim - 1) + [(0, width - a.shape[-1])])


def _rope_tables(pos):
    half = HEAD_DIM // 2
    inv = ROPE_THETA ** (-jnp.arange(half, dtype=F32) / half)
    ang = pos.astype(F32)[:, None] * inv[None, :]
    cos, sin = jnp.cos(ang), jnp.sin(ang)
    return jnp.concatenate([cos, cos], axis=-1), jnp.concatenate([-sin, sin], axis=-1)


def kernel(x_prompt, x_sample, cache_fox_k, cache_fox_v, cache_fox_logf, cache_moba_k, cache_moba_v,
           state_mlstm_c, state_mlstm_n, state_mlstm_m, state_ffn_conv, page_table,
           norm_mix_g, norm_ffn_g, norm_final_g,
           mlstm_w_in, mlstm_b_gates, mlstm_norm_g, mlstm_w_out,
           fox_w_in, fox_b_f, fox_w_out, moba_w_in, moba_w_out,
           ffn_w_up, ffn_conv_w, ffn_conv_b, ffn_w_down):
    B, S, D = x_prompt.shape
    Bd, T, _ = x_sample.shape
    H, Dh, dk = N_HEADS, HEAD_DIM, MLSTM_QK
    depth = norm_mix_g.shape[0]
    n_pool, page_rows = cache_fox_k.shape[1], cache_fox_k.shape[2]
    past = page_table.shape[1] * page_rows
    assert T == SUBLANES and D == D_MODEL and page_rows == LANES

    xp = x_prompt.reshape(B * S, D)
    xs = x_sample.reshape(Bd * T, D)
    fkp, fks, fvp, fvs, flp, fls = [], [], [], [], [], []
    mkp, mks, mvp, mvs = [], [], [], []
    acp, acs, anp, ans, amp, ams = [], [], [], [], [], []
    cvp, cvs = [], []

    for i in range(depth):
        kind, slot = i % N_MIXERS, i // N_MIXERS
        g_mix = norm_mix_g[i]
        if kind == 0:
            w = mlstm_w_in[slot]
            wq = w[:, :H * dk] * (dk ** -0.5)
            wk = w[:, H * dk:2 * H * dk]
            rest = w[:, 2 * H * dk:2 * H * dk + 2 * H * Dh]
            wig = _pad_lanes(w[:, 2 * H * dk + 2 * H * Dh:2 * H * dk + 2 * H * Dh + H])
            wfg = _pad_lanes(w[:, 2 * H * dk + 2 * H * Dh + H:])
            pad_heads = lambda a: _pad_lanes(a.reshape(D, H, dk), Dh).reshape(D, H * Dh)
            w_p = jnp.concatenate([pad_heads(wq), pad_heads(wk), rest, wig, wfg], axis=1).astype(BF16)
            w_s = jnp.concatenate([wq, wk, rest, wig, wfg], axis=1).astype(BF16)
            bi = _pad_lanes(mlstm_b_gates[slot][:H].reshape(1, H))
            bfg = _pad_lanes(mlstm_b_gates[slot][H:].reshape(1, H))
            ng = mlstm_norm_g[slot].reshape(1, H * Dh)
            wo = mlstm_w_out[slot].astype(BF16)

            q, k, v, o, ig, fg = _norm_proj(xp, g_mix, w_p, [H * Dh, H * Dh, H * Dh, H * Dh, LANES, LANES])
            hp, c_, n_, m_ = _mlstm_prompt(q, k, v, o, ig, fg, bi, bfg, ng, B)
            acp.append(c_); anp.append(n_[:, :, :dk]); amp.append(m_.reshape(B, H))
            xp = _proj_res(hp, wo, xp)

            q, k, v, o, ig, fg = _norm_proj(xs, g_mix, w_s, [H * dk, H * dk, H * Dh, H * Dh, LANES, LANES])
            hs_, c_, n_, m_ = _mlstm_sample(q, k, v, o, ig, fg, bi, bfg, ng,
                                            state_mlstm_c[slot], state_mlstm_n[slot], state_mlstm_m[slot])
            acs.append(c_); ans.append(n_.reshape(Bd, H, dk)); ams.append(m_.reshape(Bd, H))
            xs = _proj_res(hs_, wo, xs)
        elif kind == 1:
            w = fox_w_in[slot]
            w_b = jnp.concatenate([w[:, :3 * H * Dh], _pad_lanes(w[:, 3 * H * Dh:])], axis=1).astype(BF16)
            bfg = _pad_lanes(fox_b_f[slot].reshape(1, H))
            wo = fox_w_out[slot].astype(BF16)
            widths = [H * Dh, H * Dh, H * Dh, LANES]

            q, k, v, fraw = _norm_proj(xp, g_mix, w_b, widths)
            lf, ccol, crow = _fox_gates(fraw, bfg, B)
            op = _fox_prompt(q, k, v, ccol, crow, B)
            fkp.append(k.reshape(B, S, H, Dh)); fvp.append(v.reshape(B, S, H, Dh))
            flp.append(lf[:, :H].reshape(B, S, H))
            xp = _proj_res(op, wo, xp)

            q, k, v, fraw = _norm_proj(xs, g_mix, w_b, widths)
            os_, lf = _fox_sample(page_table, q, k, v, fraw, bfg,
                                  cache_fox_k[slot].reshape(n_pool, page_rows, H * Dh),
                                  cache_fox_v[slot].reshape(n_pool, page_rows, H * Dh),
                                  cache_fox_logf[slot].astype(F32).transpose(0, 2, 1))
            fks.append(k.reshape(Bd, T, H, Dh)); fvs.append(v.reshape(Bd, T, H, Dh))
            fls.append(lf[:, :H].reshape(Bd, T, H))
            xs = _proj_res(os_, wo, xs)
        else:
            w_b = moba_w_in[slot].astype(BF16)
            wo = moba_w_out[slot].astype(BF16)
            widths = [H * Dh, H * Dh, H * Dh]
            cos_p, sin_p = _rope_tables(jnp.arange(S, dtype=jnp.int32))
            cos_s, sin_s = _rope_tables(past + jnp.arange(T, dtype=jnp.int32))
            reps = min(MOBA_BLOCK, Bd * T) // T
            cos_s, sin_s = jnp.tile(cos_s, (reps, 1)), jnp.tile(sin_s, (reps, 1))

            q, k, v = _norm_proj(xp, g_mix, w_b, widths)
            q, k, km = _rope(q, k, cos_p, sin_p)
            nblk = S // MOBA_BLOCK
            km = jnp.pad(km.reshape(B, nblk, D), ((0, 0), (0, LANES - nblk), (0, 0)))
            op = _moba_prompt(q, k, v, km, B)
            mkp.append(k.reshape(B, S, H, Dh)); mvp.append(v.reshape(B, S, H, Dh))
            xp = _proj_res(op, wo, xp)

            q, k, v = _norm_proj(xs, g_mix, w_b, widths)
            q, k, _ = _rope(q, k, cos_s, sin_s)
            os_ = _moba_sample(page_table, q, k, v,
                               cache_moba_k[slot].astype(F32).reshape(n_pool, page_rows, H * Dh),
                               cache_moba_v[slot].astype(F32).reshape(n_pool, page_rows, H * Dh))
            mks.append(k.reshape(Bd, T, H, Dh)); mvs.append(v.reshape(Bd, T, H, Dh))
            xs = _proj_res(os_, wo, xs)

        wup = ffn_w_up[i].astype(BF16)
        wdn = ffn_w_down[i].astype(BF16)
        xp, tail = _ffn_prompt(xp, norm_ffn_g[i], wup, ffn_conv_w[i], ffn_conv_b[i], wdn, B)
        cvp.append(tail.reshape(B, SUBLANES, 2 * D_FF)[:, SUBLANES - (CONV_W - 1):])
        p2 = jnp.pad(state_ffn_conv[i], ((0, 0), (0, T - (CONV_W - 1)), (0, 0))).reshape(Bd * T, 2 * D_FF)
        xs, u = _ffn_sample(xs, norm_ffn_g[i], wup, ffn_conv_w[i], ffn_conv_b[i], wdn, p2)
        cvs.append(u.reshape(Bd, T, 2 * D_FF)[:, T - (CONV_W - 1):])

    y_prompt = _final_norm(xp, norm_final_g).reshape(B, S, D)
    y_sample = _final_norm(xs, norm_final_g).reshape(Bd, T, D)
    return (y_prompt, y_sample,
            jnp.stack(fkp), jnp.stack(fks), jnp.stack(fvp), jnp.stack(fvs), jnp.stack(flp), jnp.stack(fls),
            jnp.stack(mkp), jnp.stack(mks), jnp.stack(mvp), jnp.stack(mvs),
            jnp.stack(acp), jnp.stack(acs), jnp.stack(anp), jnp.stack(ans), jnp.stack(amp), jnp.stack(ams),
            jnp.stack(cvp), jnp.stack(cvs))
```

```python
import functools

import jax
import jax.numpy as jnp
from jax import lax
from jax.experimental import pallas as pl
from jax.experimental.pallas import tpu as pltpu

F32 = jnp.float32
BF16 = jnp.bfloat16
HI = lax.Precision.HIGHEST

D_MODEL = 1024
N_HEADS = 8
HEAD_DIM = 128
MLSTM_QK = 64
D_FF = 2816
CONV_W = 3
N_MIXERS = 3
MOBA_BLOCK = 256
MOBA_TOPK = 3
ROPE_THETA = 10000.0
RMS_EPS = 1e-6
NEG_INF = float("-inf")
M_FLOOR = -1e30
LANES = 128
SUBLANES = 8
VMEM_LIMIT = 56 * 1024 * 1024

NT_DIMS = (((1,), (1,)), ((), ()))
TN_DIMS = (((0,), (0,)), ((), ()))


def _params(*sem):
    return pltpu.CompilerParams(dimension_semantics=sem, vmem_limit_bytes=VMEM_LIMIT)


def _resident(shape):
    nd = len(shape)
    return pl.BlockSpec(shape, lambda *_: (0,) * nd, pipeline_mode=pl.Buffered(1))


def _rms(x, g):
    return x * lax.rsqrt(jnp.mean(x * x, axis=-1, keepdims=True) + RMS_EPS) * g


def _iota(shape, axis):
    return lax.broadcasted_iota(jnp.int32, shape, axis)


def _eye_rows(n):
    return (_iota((n, LANES), 0) == _iota((n, LANES), 1)).astype(F32)


def _nt(a, b, precision=None):
    return lax.dot_general(a, b, NT_DIMS, precision=precision, preferred_element_type=F32)


def _norm_proj_kernel(x_ref, g_ref, w_ref, *o_refs, widths):
    xn = _rms(x_ref[...], g_ref[...]).astype(BF16)
    off = 0
    for o_ref, wd in zip(o_refs, widths):
        for c in range(0, wd, 512):
            cw = min(512, wd - c)
            o_ref[:, c:c + cw] = jnp.dot(xn, w_ref[:, off + c:off + c + cw],
                                         preferred_element_type=F32)
        off += wd


def _norm_proj(x, g, w, widths, tm=256):
    n = x.shape[0]
    tm = min(tm, n)
    assert sum(widths) == w.shape[1] and n % tm == 0
    return pl.pallas_call(
        functools.partial(_norm_proj_kernel, widths=tuple(widths)),
        grid=(n // tm,),
        in_specs=[pl.BlockSpec((tm, D_MODEL), lambda i: (i, 0)),
                  _resident((1, D_MODEL)),
                  _resident(w.shape)],
        out_specs=[pl.BlockSpec((tm, wd), lambda i: (i, 0)) for wd in widths],
        out_shape=[jax.ShapeDtypeStruct((n, wd), F32) for wd in widths],
        compiler_params=_params("parallel"),
        name="norm_proj",
    )(x, g.reshape(1, D_MODEL), w)


def _proj_res_kernel(h_ref, w_ref, x_ref, o_ref):
    o_ref[...] = x_ref[...] + jnp.dot(h_ref[...].astype(BF16), w_ref[...],
                                      preferred_element_type=F32)


def _proj_res(h, w, x, tm=512):
    n = x.shape[0]
    tm = min(tm, n)
    return pl.pallas_call(
        _proj_res_kernel,
        grid=(n // tm,),
        in_specs=[pl.BlockSpec((tm, h.shape[1]), lambda i: (i, 0)),
                  _resident(w.shape),
                  pl.BlockSpec((tm, D_MODEL), lambda i: (i, 0))],
        out_specs=pl.BlockSpec((tm, D_MODEL), lambda i: (i, 0)),
        out_shape=jax.ShapeDtypeStruct((n, D_MODEL), F32),
        compiler_params=_params("parallel"),
        name="proj_res",
    )(h, w, x)


def _final_norm_kernel(x_ref, g_ref, o_ref):
    o_ref[...] = _rms(x_ref[...], g_ref[...])


def _final_norm(x, g, tm=512):
    n = x.shape[0]
    tm = min(tm, n)
    return pl.pallas_call(
        _final_norm_kernel,
        grid=(n // tm,),
        in_specs=[pl.BlockSpec((tm, D_MODEL), lambda i: (i, 0)), _resident((1, D_MODEL))],
        out_specs=pl.BlockSpec((tm, D_MODEL), lambda i: (i, 0)),
        out_shape=jax.ShapeDtypeStruct((n, D_MODEL), F32),
        compiler_params=_params("parallel"),
        name="final_norm",
    )(x, g.reshape(1, D_MODEL))


FFN_CH = 256


def _ffn_kernel(*refs, tt, grouped):
    if grouped:
        x_ref, g_ref, wup_ref, cw_ref, cb_ref, wdn_ref, p2_ref, o_ref, u_ref, hbuf = refs
    else:
        x_ref, g_ref, wup_ref, cw_ref, cb_ref, wdn_ref, o_ref, tail_ref, hbuf, carry = refs

        @pl.when(pl.program_id(1) == 0)
        def _():
            carry[...] = jnp.zeros_like(carry)

    x = x_ref[...]
    xn = _rms(x, g_ref[...]).astype(BF16)
    row = _iota((tt, 1), 0)
    for c in range(D_FF // FFN_CH):
        ys = []
        for part in range(2):
            c0 = part * D_FF + c * FFN_CH
            cols = slice(c0, c0 + FFN_CH)
            u = jnp.dot(xn, wup_ref[:, cols], preferred_element_type=F32)
            r1 = pltpu.roll(u, 1, 0)
            r2 = pltpu.roll(u, 2, 0)
            if grouped:
                p2 = p2_ref[:, cols]
                p1 = pltpu.roll(p2, tt - 1, 0)
                t8 = row & (SUBLANES - 1)
                um1 = jnp.where(t8 < 1, p1, r1)
                um2 = jnp.where(t8 < 2, p2, r2)
                u_ref[:, cols] = u
            else:
                pc = carry[:, cols]
                row8 = row[:SUBLANES]
                f1 = jnp.where(row8 < 1, pltpu.roll(pc, 1, 0), r1[:SUBLANES])
                f2 = jnp.where(row8 < 2, pltpu.roll(pc, 2, 0), r2[:SUBLANES])
                um1 = jnp.concatenate([f1, r1[SUBLANES:]], axis=0)
                um2 = jnp.concatenate([f2, r2[SUBLANES:]], axis=0)
                tail = u[tt - SUBLANES:]
                carry[:, cols] = tail
                tail_ref[:, cols] = tail
            y = cb_ref[:, cols] + cw_ref[0:1, cols] * um2
            y = y + cw_ref[1:2, cols] * um1
            y = y + cw_ref[2:3, cols] * u
            ys.append(y)
        gate, val = ys
        hbuf[:, c * FFN_CH:(c + 1) * FFN_CH] = (gate * jax.nn.sigmoid(gate) * val).astype(BF16)
    o_ref[...] = x + jnp.dot(hbuf[...], wdn_ref[...], preferred_element_type=F32)


def _ffn_prompt(x, g, wup, cw, cb, wdn, nseq, tt=512):
    n = x.shape[0]
    nt = n // nseq // tt
    return pl.pallas_call(
        functools.partial(_ffn_kernel, tt=tt, grouped=False),
        grid=(nseq, nt),
        in_specs=[pl.BlockSpec((tt, D_MODEL), lambda b, t: (b * nt + t, 0)),
                  _resident((1, D_MODEL)), _resident(wup.shape), _resident(cw.shape),
                  _resident((1, 2 * D_FF)), _resident(wdn.shape)],
        out_specs=[pl.BlockSpec((tt, D_MODEL), lambda b, t: (b * nt + t, 0)),
                   pl.BlockSpec((SUBLANES, 2 * D_FF), lambda b, t: (b, 0))],
        out_shape=[jax.ShapeDtypeStruct((n, D_MODEL), F32),
                   jax.ShapeDtypeStruct((nseq * SUBLANES, 2 * D_FF), F32)],
        scratch_shapes=[pltpu.VMEM((tt, D_FF), BF16), pltpu.VMEM((SUBLANES, 2 * D_FF), F32)],
        compiler_params=_params("parallel", "arbitrary"),
        name="ffn_prompt",
    )(x, g.reshape(1, D_MODEL), wup, cw, cb.reshape(1, 2 * D_FF), wdn)


def _ffn_sample(x, g, wup, cw, cb, wdn, p2, tt=128):
    n = x.shape[0]
    tt = min(tt, n)
    return pl.pallas_call(
        functools.partial(_ffn_kernel, tt=tt, grouped=True),
        grid=(n // tt,),
        in_specs=[pl.BlockSpec((tt, D_MODEL), lambda i: (i, 0)),
                  _resident((1, D_MODEL)), _resident(wup.shape), _resident(cw.shape),
                  _resident((1, 2 * D_FF)), _resident(wdn.shape),
                  pl.BlockSpec((tt, 2 * D_FF), lambda i: (i, 0))],
        out_specs=[pl.BlockSpec((tt, D_MODEL), lambda i: (i, 0)),
                   pl.BlockSpec((tt, 2 * D_FF), lambda i: (i, 0))],
        out_shape=[jax.ShapeDtypeStruct((n, D_MODEL), F32),
                   jax.ShapeDtypeStruct((n, 2 * D_FF), F32)],
        scratch_shapes=[pltpu.VMEM((tt, D_FF), BF16)],
        compiler_params=_params("parallel"),
        name="ffn_sample",
    )(x, g.reshape(1, D_MODEL), wup, cw, cb.reshape(1, 2 * D_FF), wdn, p2)


def _mlstm_prompt_kernel(q_ref, k_ref, v_ref, o_ref, ig_ref, fg_ref, bi_ref, bf_ref, ng_ref,
                         h_ref, cout_ref, nout_ref, mout_ref, c_s, n_s, m_s, *, chunk):
    L = chunk
    j = pl.program_id(1)

    @pl.when(j == 0)
    def _():
        c_s[...] = jnp.zeros_like(c_s)
        n_s[...] = jnp.zeros_like(n_s)
        m_s[...] = jnp.zeros_like(m_s)

    ig = ig_ref[...] + bi_ref[...]
    lf = jax.nn.log_sigmoid(fg_ref[...] + bf_ref[...])
    tril = (_iota((L, L), 0) >= _iota((L, L), 1))
    bt = jnp.dot(tril.astype(F32), lf, precision=HI, preferred_element_type=F32)
    rc = ig - bt
    r_t = _nt(_eye_rows(SUBLANES), rc, HI)
    for h in range(N_HEADS):
        hs = slice(h * HEAD_DIM, (h + 1) * HEAD_DIM)
        qh, kh, vh = q_ref[:, hs], k_ref[:, hs], v_ref[:, hs]
        qb, kb, vb = qh.astype(BF16), kh.astype(BF16), vh.astype(BF16)
        bcol = bt[:, h:h + 1]
        dm = jnp.where(tril, bcol + r_t[h:h + 1, :], NEG_INF)
        m_prev = m_s[h][:, 0:1]
        inter = bcol + m_prev
        mt = jnp.maximum(jnp.max(dm, axis=-1, keepdims=True), inter)
        w_intra = jnp.exp(dm - mt)
        w_inter = jnp.exp(inter - mt)
        c_prev = c_s[h]
        n_prev = n_s[h]
        s = _nt(qb, kb) * w_intra
        num = (jnp.dot(s.astype(BF16), vb, preferred_element_type=F32)
               + jnp.dot(qb, c_prev.astype(BF16), preferred_element_type=F32) * w_inter)
        den = (jnp.sum(s, axis=-1, keepdims=True)
               + jnp.sum(qh * n_prev, axis=-1, keepdims=True) * w_inter)
        den = jnp.maximum(jnp.abs(den), jnp.exp(-mt))
        hh = num / den
        hh = hh * lax.rsqrt(jnp.mean(hh * hh, axis=-1, keepdims=True) + RMS_EPS) * ng_ref[:, hs]
        h_ref[:, hs] = hh * jax.nn.sigmoid(o_ref[:, hs])
        b_last = bcol[L - 1:L, :]
        gcol = b_last + rc[:, h:h + 1]
        m_new = jnp.maximum(b_last + m_prev, jnp.max(gcol, axis=0, keepdims=True))
        a_prev = jnp.exp(b_last + m_prev - m_new)
        ka = kh * jnp.exp(gcol - m_new)
        c_new = a_prev * c_prev + lax.dot_general(ka.astype(BF16), vb, TN_DIMS,
                                                  preferred_element_type=F32)
        n_new = a_prev * n_prev + jnp.sum(ka, axis=0, keepdims=True)
        c_s[h] = c_new
        n_s[h] = n_new
        m_s[h] = jnp.broadcast_to(m_new, (1, LANES))

    @pl.when(j == pl.num_programs(1) - 1)
    def _():
        for h in range(N_HEADS):
            cout_ref[0, h] = c_s[h][:MLSTM_QK, :]
            nout_ref[0, h:h + 1, :] = n_s[h]
            mout_ref[0, :, h:h + 1] = m_s[h][:, 0:1]


def _mlstm_prompt(q, k, v, o, ig, fg, bi, bfg, ng, nseq, chunk=128):
    n = q.shape[0]
    nc = n // nseq // chunk
    wide = pl.BlockSpec((chunk, D_MODEL), lambda b, j: (b * nc + j, 0))
    narrow = pl.BlockSpec((chunk, LANES), lambda b, j: (b * nc + j, 0))
    return pl.pallas_call(
        functools.partial(_mlstm_prompt_kernel, chunk=chunk),
        grid=(nseq, nc),
        in_specs=[wide, wide, wide, wide, narrow, narrow,
                  _resident((1, LANES)), _resident((1, LANES)), _resident((1, D_MODEL))],
        out_specs=[wide,
                   pl.BlockSpec((1, N_HEADS, MLSTM_QK, HEAD_DIM), lambda b, j: (b, 0, 0, 0)),
                   pl.BlockSpec((1, N_HEADS, LANES), lambda b, j: (b, 0, 0)),
                   pl.BlockSpec((1, 1, N_HEADS), lambda b, j: (b, 0, 0))],
        out_shape=[jax.ShapeDtypeStruct((n, D_MODEL), F32),
                   jax.ShapeDtypeStruct((nseq, N_HEADS, MLSTM_QK, HEAD_DIM), F32),
                   jax.ShapeDtypeStruct((nseq, N_HEADS, LANES), F32),
                   jax.ShapeDtypeStruct((nseq, 1, N_HEADS), F32)],
        scratch_shapes=[pltpu.VMEM((N_HEADS, HEAD_DIM, HEAD_DIM), F32),
                        pltpu.VMEM((N_HEADS, 1, LANES), F32),
                        pltpu.VMEM((N_HEADS, 1, LANES), F32)],
        compiler_params=_params("parallel", "arbitrary"),
        name="mlstm_prompt",
    )(q, k, v, o, ig, fg, bi, bfg, ng)


def _cumsum_rows8(x):
    row = _iota(x.shape, 0)
    for sh in (1, 2, 4):
        x = x + jnp.where(row >= sh, pltpu.roll(x, sh, 0), 0.0)
    return x


def _stack_cols(x, n=N_HEADS):
    return jnp.concatenate([x[:, h:h + 1] for h in range(n)], axis=0)


def _stack_bcast(x, rows, n=N_HEADS):
    return jnp.concatenate([jnp.broadcast_to(x[:, h:h + 1], (rows, 1)) for h in range(n)], axis=0)


def _pad_rows(x, rows=LANES):
    return jnp.concatenate([x, jnp.zeros((rows - x.shape[0], x.shape[1]), x.dtype)], axis=0)


def _mlstm_sample_kernel(q_ref, k_ref, v_ref, o_ref, ig_ref, fg_ref, bi_ref, bf_ref, ng_ref,
                         c0_ref, n0_ref, m0_ref, h_ref, cout_ref, nout_ref, mout_ref):
    T, HT, QW = SUBLANES, N_HEADS * SUBLANES, N_HEADS * MLSTM_QK
    ig = ig_ref[...] + bi_ref[...]
    lf = jax.nn.log_sigmoid(fg_ref[...] + bf_ref[...])
    bt = _cumsum_rows8(lf)
    rc = ig - bt
    m0 = m0_ref[0]
    bcol = _stack_cols(bt)
    r_t = _nt(_eye_rows(SUBLANES), _pad_rows(rc), HI)
    rrow = jnp.concatenate([jnp.broadcast_to(r_t[h:h + 1, :], (T, LANES))
                            for h in range(N_HEADS)], axis=0)
    lane = _iota((HT, LANES), 1)
    tok = _iota((HT, LANES), 0) & (T - 1)
    dm = jnp.where(lane <= tok, bcol + rrow, NEG_INF)
    inter = bcol + _stack_bcast(m0, T)
    mt = jnp.maximum(jnp.max(dm, axis=-1, keepdims=True), inter)
    w_intra = jnp.exp(dm - mt)
    w_inter = jnp.exp(inter - mt)

    head_of_row = _iota((HT, QW), 0) // T
    head_of_lane = _iota((HT, QW), 1) // MLSTM_QK
    diag = head_of_row == head_of_lane
    q8, k8, v8 = q_ref[...], k_ref[...], v_ref[...]
    qbd = jnp.where(diag, jnp.concatenate([q8] * N_HEADS, axis=0), 0.0)
    qbd_b = qbd.astype(BF16)
    s = _nt(qbd_b, _pad_rows(k8).astype(BF16)) * w_intra
    nv = jnp.dot(s.astype(BF16), _pad_rows(v8).astype(BF16), preferred_element_type=F32)
    row_head = _iota((HT, HEAD_DIM), 0) // T
    num = jnp.zeros((HT, HEAD_DIM), F32)
    for h in range(N_HEADS):
        num = num + jnp.where(row_head == h, nv[:, h * HEAD_DIM:(h + 1) * HEAD_DIM], 0.0)
    c_prev = c0_ref[0].reshape(QW, HEAD_DIM)
    n_prev = n0_ref[0]
    num = num + jnp.dot(qbd_b, c_prev.astype(BF16), preferred_element_type=F32) * w_inter
    den = (jnp.sum(s, axis=-1, keepdims=True)
           + jnp.sum(qbd * n_prev, axis=-1, keepdims=True) * w_inter)
    den = jnp.maximum(jnp.abs(den), jnp.exp(-mt))
    hh = num / den
    ng = jnp.concatenate([jnp.broadcast_to(ng_ref[:, h * HEAD_DIM:(h + 1) * HEAD_DIM], (T, HEAD_DIM))
                          for h in range(N_HEADS)], axis=0)
    o8 = o_ref[...]
    ost = jnp.concatenate([o8[:, h * HEAD_DIM:(h + 1) * HEAD_DIM] for h in range(N_HEADS)], axis=0)
    hh = hh * lax.rsqrt(jnp.mean(hh * hh, axis=-1, keepdims=True) + RMS_EPS) * ng
    hh = hh * jax.nn.sigmoid(ost)
    h_ref[...] = jnp.concatenate([hh[h * T:(h + 1) * T, :] for h in range(N_HEADS)], axis=1)

    b_last = bt[T - 1:T, :]
    g2 = b_last + rc
    m_new = jnp.maximum(b_last + m0, jnp.max(g2, axis=0, keepdims=True))
    a_prev = jnp.exp(b_last + m0 - m_new)
    a_tok = jnp.exp(g2 - m_new)
    kabd = jnp.where(diag, jnp.concatenate([k8] * N_HEADS, axis=0) * _stack_cols(a_tok), 0.0)
    vst = jnp.concatenate([v8[:, h * HEAD_DIM:(h + 1) * HEAD_DIM] for h in range(N_HEADS)], axis=0)
    dc = lax.dot_general(kabd.astype(BF16), vst.astype(BF16), TN_DIMS, preferred_element_type=F32)
    c_new = _stack_bcast(a_prev, MLSTM_QK) * c_prev + dc
    cout_ref[0] = c_new.reshape(N_HEADS, MLSTM_QK, HEAD_DIM)
    a_lane = jnp.max(jnp.where(diag, _stack_bcast(a_prev, T), 0.0), axis=0, keepdims=True)
    nout_ref[0] = a_lane * n_prev + jnp.sum(kabd, axis=0, keepdims=True)
    mout_ref[0] = m_new[:, :N_HEADS]


def _mlstm_sample(q, k, v, o, ig, fg, bi, bfg, ng, c0, n0, m0):
    n = q.shape[0]
    nb = n // SUBLANES
    qw = N_HEADS * MLSTM_QK

    def rows(w):
        return pl.BlockSpec((SUBLANES, w), lambda b: (b, 0))

    cspec = pl.BlockSpec((1, N_HEADS, MLSTM_QK, HEAD_DIM), lambda b: (b, 0, 0, 0))
    nspec = pl.BlockSpec((1, 1, qw), lambda b: (b, 0, 0))
    mspec = pl.BlockSpec((1, 1, N_HEADS), lambda b: (b, 0, 0))
    return pl.pallas_call(
        _mlstm_sample_kernel,
        grid=(nb,),
        in_specs=[rows(qw), rows(qw), rows(D_MODEL), rows(D_MODEL), rows(LANES), rows(LANES),
                  _resident((1, LANES)), _resident((1, LANES)), _resident((1, D_MODEL)),
                  cspec, nspec, pl.BlockSpec((1, 1, LANES), lambda b: (b, 0, 0))],
        out_specs=[rows(D_MODEL), cspec, nspec, mspec],
        out_shape=[jax.ShapeDtypeStruct((n, D_MODEL), F32),
                   jax.ShapeDtypeStruct((nb, N_HEADS, MLSTM_QK, HEAD_DIM), F32),
                   jax.ShapeDtypeStruct((nb, 1, qw), F32),
                   jax.ShapeDtypeStruct((nb, 1, N_HEADS), F32)],
        compiler_params=_params("parallel"),
        name="mlstm_sample",
    )(q, k, v, o, ig, fg, bi, bfg, ng, c0, n0.reshape(nb, 1, qw), _pad_lanes(m0).reshape(nb, 1, LANES))


def _fox_gate_kernel(f_ref, bf_ref, lf_ref, crow_ref, carry, *, ts):
    @pl.when(pl.program_id(1) == 0)
    def _():
        carry[...] = jnp.zeros_like(carry)

    lf = jax.nn.log_sigmoid(f_ref[...] + bf_ref[...])
    tril = (_iota((ts, ts), 0) >= _iota((ts, ts), 1)).astype(F32)
    c = jnp.dot(tril, lf, precision=HI, preferred_element_type=F32) + carry[...]
    carry[...] = c[ts - 1:ts, :]
    lf_ref[...] = lf
    crow_ref[0] = _nt(_eye_rows(SUBLANES), c, HI)


def _fox_gates(fraw, bfg, nseq, ts=512):
    n = fraw.shape[0]
    seq = n // nseq
    nt = seq // ts
    tok = pl.BlockSpec((ts, LANES), lambda b, t: (b * nt + t, 0))
    return pl.pallas_call(
        functools.partial(_fox_gate_kernel, ts=ts),
        grid=(nseq, nt),
        in_specs=[tok, _resident((1, LANES))],
        out_specs=[tok, pl.BlockSpec((1, SUBLANES, ts), lambda b, t: (b, 0, t))],
        out_shape=[jax.ShapeDtypeStruct((n, LANES), F32),
                   jax.ShapeDtypeStruct((nseq, SUBLANES, seq), F32)],
        scratch_shapes=[pltpu.VMEM((1, LANES), F32)],
        compiler_params=_params("parallel", "arbitrary"),
        name="fox_gates",
    )(fraw, bfg)


def _online_update(s, vb, m_ref, l_ref, acc_ref, hs):
    s_tiles, v_tiles = (s, vb) if isinstance(s, (list, tuple)) else ([s], [vb])
    m_prev = m_ref[...]
    m_new = m_prev
    for st in s_tiles:
        m_new = jnp.maximum(m_new, jnp.max(st, axis=-1, keepdims=True))
    alpha = jnp.exp(m_prev - m_new)
    l_new = alpha * l_ref[...]
    acc = alpha * acc_ref[:, hs]
    for st, vt in zip(s_tiles, v_tiles):
        p = jnp.exp(st - m_new)
        l_new = l_new + jnp.sum(p, axis=-1, keepdims=True)
        acc = acc + jnp.dot(p.astype(BF16), vt, preferred_element_type=F32)
    l_ref[...] = l_new
    acc_ref[:, hs] = acc
    m_ref[...] = m_new


def _flash_update(s, v, m_ref, l_ref, acc_ref, hs):
    m_prev = m_ref[...]
    m_new = jnp.maximum(m_prev, jnp.max(s, axis=-1, keepdims=True))
    alpha = jnp.exp(m_prev - m_new)
    p = jnp.exp(s - jnp.concatenate([m_new] * (s.shape[1] // LANES), axis=1))
    v_ones = jnp.concatenate([v.astype(BF16), jnp.ones((v.shape[0], LANES), BF16)], axis=1)
    pv = jnp.dot(p.astype(BF16), v_ones, preferred_element_type=F32)
    acc_ref[:, hs] = alpha * acc_ref[:, hs] + pv[:, :HEAD_DIM]
    l_ref[...] = alpha * l_ref[...] + pv[:, HEAD_DIM:]
    m_ref[...] = m_new


def _fox_prompt_kernel(q_ref, k_ref, v_ref, crow_ref, o_ref, qs_s, m_s, l_s, acc_s, *, tq):
    i, j = pl.program_id(1), pl.program_id(2)

    @pl.when(j == 0)
    def _():
        qs_s[...] = (q_ref[...] * (HEAD_DIM ** -0.5)).astype(BF16)
        m_s[...] = jnp.full_like(m_s, M_FLOOR)
        l_s[...] = jnp.zeros_like(l_s)
        acc_s[...] = jnp.zeros_like(acc_s)

    def attend(diagonal):
        causal = _iota((tq, tq), 1) <= _iota((tq, tq), 0)
        for h in range(N_HEADS):
            hs = slice(h * HEAD_DIM, (h + 1) * HEAD_DIM)
            s = _nt(qs_s[:, hs], k_ref[:, hs].astype(BF16)) - crow_ref[0, h:h + 1, :]
            if diagonal:
                s = jnp.where(causal, s, NEG_INF)
            _flash_update(s, v_ref[:, hs], m_s.at[h], l_s.at[h], acc_s, hs)

    @pl.when(j < i)
    def _():
        attend(False)

    @pl.when(j == i)
    def _():
        attend(True)
        for h in range(N_HEADS):
            hs = slice(h * HEAD_DIM, (h + 1) * HEAD_DIM)
            o_ref[:, hs] = acc_s[:, hs] / l_s[h]


def _fox_prompt(q, k, v, crow, nseq, tq=512):
    n = q.shape[0]
    nq = n // nseq // tq
    qspec = pl.BlockSpec((tq, D_MODEL), lambda b, i, j: (b * nq + i, 0))
    kspec = pl.BlockSpec((tq, D_MODEL), lambda b, i, j: (b * nq + jnp.minimum(j, i), 0))
    return pl.pallas_call(
        functools.partial(_fox_prompt_kernel, tq=tq),
        grid=(nseq, nq, nq),
        in_specs=[qspec, kspec, kspec,
                  pl.BlockSpec((1, SUBLANES, tq), lambda b, i, j: (b, 0, jnp.minimum(j, i)))],
        out_specs=qspec,
        out_shape=jax.ShapeDtypeStruct((n, D_MODEL), F32),
        scratch_shapes=[pltpu.VMEM((tq, D_MODEL), BF16),
                        pltpu.VMEM((N_HEADS, tq, LANES), F32), pltpu.VMEM((N_HEADS, tq, LANES), F32),
                        pltpu.VMEM((tq, D_MODEL), F32)],
        compiler_params=_params("parallel", "parallel", "arbitrary"),
        name="fox_prompt",
    )(q, k, v, crow)


HT_ROWS = N_HEADS * SUBLANES
PAGE_ROWS = LANES * N_HEADS


def _stack_heads(x8):
    return jnp.concatenate([x8[:, h * HEAD_DIM:(h + 1) * HEAD_DIM] for h in range(N_HEADS)], axis=0)


def _unstack_heads(x):
    return jnp.concatenate([x[h * SUBLANES:(h + 1) * SUBLANES, :] for h in range(N_HEADS)], axis=1)


def _same_head_past():
    return ((_iota((HT_ROWS, PAGE_ROWS), 1) & (N_HEADS - 1))
            == (_iota((HT_ROWS, PAGE_ROWS), 0) // SUBLANES))


def _visible_new():
    row, lane = _iota((HT_ROWS, LANES), 0), _iota((HT_ROWS, LANES), 1)
    return (((lane // SUBLANES) == (row // SUBLANES))
            & ((lane & (SUBLANES - 1)) <= (row & (SUBLANES - 1))))


def _cumsum_keys(x):
    npg, w = x.shape
    lane, row = _iota(x.shape, 1), _iota(x.shape, 0)
    sh = N_HEADS
    while sh < w:
        r = pltpu.roll(x, sh, 1)
        from_prev_page = jnp.where(row >= 1, pltpu.roll(r, 1, 0), 0.0)
        x = x + jnp.where(lane >= sh, r, from_prev_page)
        sh *= 2
    sh = 1
    while sh < npg:
        x = x + jnp.where(row >= sh, pltpu.roll(x, sh, 0), 0.0)
        sh *= 2
    return x


def _fox_sample_kernel(pt_ref, q_ref, kn_ref, vn_ref, f_ref, bf_ref, lftab_ref, *rest, group):
    kc_refs, vc_refs = rest[:group], rest[group:2 * group]
    o_ref, lf_ref, q_s, m_s, l_s, acc_s, c_s = rest[2 * group:]
    b, p = pl.program_id(0), pl.program_id(1)
    npages = c_s.shape[0]
    one_head = slice(0, HEAD_DIM)

    @pl.when(p == 0)
    def _():
        q_s[...] = (_stack_heads(q_ref[...]) * (HEAD_DIM ** -0.5)).astype(BF16)
        m_s[...] = jnp.full_like(m_s, M_FLOOR)
        l_s[...] = jnp.zeros_like(l_s)
        acc_s[...] = jnp.zeros_like(acc_s)
        for pg in range(npages):
            c_s[pg:pg + 1, :] = lftab_ref[pl.ds(pt_ref[b, pg], 1), :]
        c_s[...] = _cumsum_keys(c_s[...])

    same_head = _same_head_past()
    q = q_s[...]
    s_tiles = [jnp.where(same_head,
                         _nt(q, kc_refs[g][0, 0].astype(BF16)) - c_s[pl.ds(p * group + g, 1), :],
                         NEG_INF) for g in range(group)]
    _online_update(s_tiles, [vc_refs[g][0, 0].astype(BF16) for g in range(group)],
                   m_s, l_s, acc_s, one_head)

    @pl.when(p == pl.num_programs(1) - 1)
    def _():
        lane = _iota((1, LANES), 1)
        lf = jax.nn.log_sigmoid(f_ref[0] + bf_ref[...])
        lf_ref[0] = lf
        cum = lf
        for sh in (1, 2, 4):
            cum = cum + jnp.where((lane & (SUBLANES - 1)) >= sh, pltpu.roll(cum, sh, 1), 0.0)
        tail = c_s[npages - 1:npages, PAGE_ROWS - LANES:]
        spread = ((_iota((LANES, LANES), 0) - (LANES - N_HEADS))
                  == (_iota((LANES, LANES), 1) // SUBLANES)).astype(F32)
        past = jnp.dot(jnp.broadcast_to(tail, (SUBLANES, LANES)), spread, precision=HI,
                       preferred_element_type=F32)[0:1]
        sn = _nt(q_s[...], _pad_rows(_stack_heads(kn_ref[...])).astype(BF16)) - (past + cum)
        sn = jnp.where(_visible_new(), sn, NEG_INF)
        _online_update(sn, _pad_rows(_stack_heads(vn_ref[...])).astype(BF16), m_s, l_s, acc_s, one_head)
        o_ref[...] = _unstack_heads(acc_s[...] / l_s[...])


PAGE_GROUP = 4


def _page_specs(slot, group):
    return [pl.BlockSpec((1, 1, PAGE_ROWS, HEAD_DIM),
                         lambda b, p, pt, g=g: (slot, pt[b, p * group + g], 0, 0))
            for g in range(group)]


def _fox_sample(page_table, q, kn, vn, fflat, bflat, lftab, kc, vc, slot):
    n = q.shape[0]
    nb, npages = page_table.shape
    group = PAGE_GROUP
    assert npages % group == 0

    def rows(w):
        return pl.BlockSpec((SUBLANES, w), lambda b, p, pt: (b, 0))

    pages = _page_specs(slot, group)
    flat = pl.BlockSpec((1, 1, LANES), lambda b, p, pt: (b, 0, 0))
    grid_spec = pltpu.PrefetchScalarGridSpec(
        num_scalar_prefetch=1,
        grid=(nb, npages // group),
        in_specs=[rows(D_MODEL), rows(D_MODEL), rows(D_MODEL), flat,
                  _resident((1, LANES)), _resident(lftab.shape)] + pages + pages,
        out_specs=[rows(D_MODEL), flat],
        scratch_shapes=[pltpu.VMEM((HT_ROWS, HEAD_DIM), BF16),
                        pltpu.VMEM((HT_ROWS, 1), F32), pltpu.VMEM((HT_ROWS, 1), F32),
                        pltpu.VMEM((HT_ROWS, HEAD_DIM), F32),
                        pltpu.VMEM((npages, PAGE_ROWS), F32)])
    return pl.pallas_call(
        functools.partial(_fox_sample_kernel, group=group),
        grid_spec=grid_spec,
        out_shape=[jax.ShapeDtypeStruct((n, D_MODEL), F32), jax.ShapeDtypeStruct((nb, 1, LANES), F32)],
        compiler_params=_params("parallel", "arbitrary"),
        name="fox_sample",
    )(page_table, q, kn, vn, fflat, bflat, lftab, *([kc] * group), *([vc] * group))


def _rope_kernel(q_ref, k_ref, cos_ref, sin_ref, qo_ref, ko_ref, km_ref):
    cos, sin = cos_ref[...], sin_ref[...]
    for h in range(N_HEADS):
        hs = slice(h * HEAD_DIM, (h + 1) * HEAD_DIM)
        for src, dst in ((q_ref, qo_ref), (k_ref, ko_ref)):
            x = src[:, hs]
            dst[:, hs] = x * cos + pltpu.roll(x, HEAD_DIM // 2, 1) * sin
    km_ref[0] = jnp.mean(ko_ref[...], axis=0, keepdims=True)


def _rope(q, k, cos, sin, tt=MOBA_BLOCK):
    n = q.shape[0]
    tt = min(tt, n)
    ntab = cos.shape[0] // tt
    tok = pl.BlockSpec((tt, D_MODEL), lambda i: (i, 0))
    tab = pl.BlockSpec((tt, HEAD_DIM), lambda i: (i % ntab, 0))
    return pl.pallas_call(
        _rope_kernel,
        grid=(n // tt,),
        in_specs=[tok, tok, tab, tab],
        out_specs=[tok, tok, pl.BlockSpec((1, 1, D_MODEL), lambda i: (i, 0, 0))],
        out_shape=[jax.ShapeDtypeStruct((n, D_MODEL), F32), jax.ShapeDtypeStruct((n, D_MODEL), F32),
                   jax.ShapeDtypeStruct((n // tt, 1, D_MODEL), F32)],
        compiler_params=_params("parallel"),
        name="rope",
    )(q, k, cos, sin)


def _top_blocks(gate, n_valid):
    lane = _iota(gate.shape, 1)
    g = jnp.where(lane < n_valid, gate, NEG_INF)
    sel = jnp.zeros(gate.shape, F32)
    for _ in range(MOBA_TOPK):
        mx = jnp.max(g, axis=-1, keepdims=True)
        idx = jnp.min(jnp.where(g == mx, lane, LANES), axis=-1, keepdims=True)
        hit = lane == idx
        sel = jnp.where(hit & (mx > NEG_INF), 1.0, sel)
        g = jnp.where(hit, NEG_INF, g)
    return sel


def _top_blocks_t(gate_t, n_valid):
    nb = gate_t.shape[0]
    blk = _iota(gate_t.shape, 0)
    g = jnp.where(blk < n_valid, gate_t, NEG_INF)
    sel = jnp.zeros(gate_t.shape, F32)
    for _ in range(MOBA_TOPK):
        mx = jnp.max(g, axis=0, keepdims=True)
        idx = jnp.min(jnp.where(g == mx, blk, nb), axis=0, keepdims=True)
        hit = blk == idx
        sel = jnp.where(hit & (mx > NEG_INF), 1.0, sel)
        g = jnp.where(hit, NEG_INF, g)
    return sel


MOBA_TILE = 2 * MOBA_BLOCK


def _moba_prompt_kernel(q_ref, k_ref, v_ref, km_ref, o_ref, qs_s, sel_s, m_s, l_s, acc_s):
    i, j = pl.program_id(1), pl.program_id(2)
    tq = MOBA_TILE
    blocks_per_tile = tq // MOBA_BLOCK

    @pl.when(j == 0)
    def _():
        qs_s[...] = (q_ref[...] * (HEAD_DIM ** -0.5)).astype(BF16)
        m_s[...] = jnp.full_like(m_s, M_FLOOR)
        l_s[...] = jnp.zeros_like(l_s)
        acc_s[...] = jnp.zeros_like(acc_s)
        own = i * blocks_per_tile + _iota((1, tq), 1) // MOBA_BLOCK
        for h in range(N_HEADS):
            hs = slice(h * HEAD_DIM, (h + 1) * HEAD_DIM)
            sel_t = _top_blocks_t(_nt(km_ref[0, :, hs], q_ref[:, hs], HI), own)
            sel_s[h] = _pad_rows(sel_t).T.astype(BF16)

    def attend(diagonal):
        key_block = j * blocks_per_tile + _iota((LANES, tq), 1) // MOBA_BLOCK
        spread = (_iota((LANES, tq), 0) == key_block).astype(BF16)
        if diagonal:
            row, col = _iota((tq, tq), 0), _iota((tq, tq), 1)
            own_causal = ((row // MOBA_BLOCK) == (col // MOBA_BLOCK)) & (col <= row)
        for h in range(N_HEADS):
            hs = slice(h * HEAD_DIM, (h + 1) * HEAD_DIM)
            visible = jnp.dot(sel_s[h], spread, preferred_element_type=F32) > 0.5
            if diagonal:
                visible = visible | own_causal
            s = jnp.where(visible, _nt(qs_s[:, hs], k_ref[:, hs].astype(BF16)), NEG_INF)
            _flash_update(s, v_ref[:, hs], m_s.at[h], l_s.at[h], acc_s, hs)

    @pl.when(j < i)
    def _():
        attend(False)

    @pl.when(j == i)
    def _():
        attend(True)
        for h in range(N_HEADS):
            hs = slice(h * HEAD_DIM, (h + 1) * HEAD_DIM)
            o_ref[:, hs] = acc_s[:, hs] / l_s[h]


def _moba_prompt(q, k, v, kmean, nseq):
    n = q.shape[0]
    tq = MOBA_TILE
    nq = n // nseq // tq
    assert kmean.shape[1] % SUBLANES == 0 and kmean.shape[1] <= LANES
    qspec = pl.BlockSpec((tq, D_MODEL), lambda b, i, j: (b * nq + i, 0))
    kspec = pl.BlockSpec((tq, D_MODEL), lambda b, i, j: (b * nq + jnp.minimum(j, i), 0))
    return pl.pallas_call(
        _moba_prompt_kernel,
        grid=(nseq, nq, nq),
        in_specs=[qspec, kspec, kspec,
                  pl.BlockSpec((1, kmean.shape[1], D_MODEL), lambda b, i, j: (b, 0, 0))],
        out_specs=qspec,
        out_shape=jax.ShapeDtypeStruct((n, D_MODEL), F32),
        scratch_shapes=[pltpu.VMEM((tq, D_MODEL), BF16), pltpu.VMEM((N_HEADS, tq, LANES), BF16),
                        pltpu.VMEM((N_HEADS, tq, LANES), F32), pltpu.VMEM((N_HEADS, tq, LANES), F32),
                        pltpu.VMEM((tq, D_MODEL), F32)],
        compiler_params=_params("parallel", "parallel", "arbitrary"),
        name="moba_prompt",
    )(q, k, v, kmean)


def _moba_sample_kernel(pt_ref, q_ref, kn_ref, vn_ref, *rest, group, pages_per_block):
    kc_refs, vc_refs = rest[:group], rest[group:2 * group]
    o_ref, qf_s, qb_s, m_all, l_all, acc_all, ksum_s = rest[2 * group:]
    p = pl.program_id(1)
    n_blocks = acc_all.shape[0]
    blocks_per_step = group // pages_per_block

    @pl.when(p == 0)
    def _():
        q = _stack_heads(q_ref[...])
        qf_s[...] = q
        qb_s[...] = (q * (HEAD_DIM ** -0.5)).astype(BF16)
        ksum_s[...] = jnp.zeros_like(ksum_s)

    same_head = _same_head_past()
    q = qb_s[...]
    for bi in range(blocks_per_step):
        blk = p * blocks_per_step + bi
        ksum = jnp.zeros((N_HEADS, HEAD_DIM), F32)
        s_tiles, v_tiles = [], []
        for g in range(bi * pages_per_block, (bi + 1) * pages_per_block):
            kpage = kc_refs[g][0, 0]
            ksum = ksum + jnp.sum(kpage.reshape(LANES, N_HEADS, HEAD_DIM), axis=0)
            s_tiles.append(jnp.where(same_head, _nt(q, kpage.astype(BF16)), NEG_INF))
            v_tiles.append(vc_refs[g][0, 0].astype(BF16))
        ksum_s[pl.ds(pl.multiple_of(blk * N_HEADS, N_HEADS), N_HEADS), :] = ksum
        mp = s_tiles[0].max(axis=-1, keepdims=True)
        for st in s_tiles[1:]:
            mp = jnp.maximum(mp, jnp.max(st, axis=-1, keepdims=True))
        lsum = jnp.zeros((HT_ROWS, 1), F32)
        acc = jnp.zeros((HT_ROWS, HEAD_DIM), F32)
        for st, vt in zip(s_tiles, v_tiles):
            e = jnp.exp(st - mp)
            lsum = lsum + jnp.sum(e, axis=-1, keepdims=True)
            acc = acc + jnp.dot(e.astype(BF16), vt, preferred_element_type=F32)
        m_all[blk] = mp
        l_all[blk] = lsum
        acc_all[blk] = acc

    @pl.when(p == pl.num_programs(1) - 1)
    def _():
        kmean = ksum_s[...] * (1.0 / MOBA_BLOCK)
        g = _nt(qf_s[...], kmean, HI)
        same = (_iota((HT_ROWS, LANES), 1) & (N_HEADS - 1)) == (_iota((HT_ROWS, LANES), 0) // SUBLANES)
        group = ((_iota((LANES, LANES), 0) // N_HEADS) == _iota((LANES, LANES), 1)).astype(F32)
        gate = jnp.dot(jnp.where(same, g, 0.0), group, precision=HI, preferred_element_type=F32)
        sel = _top_blocks(gate, n_blocks)
        so = _nt(qb_s[...], _pad_rows(_stack_heads(kn_ref[...])).astype(BF16))
        so = jnp.where(_visible_new(), so, NEG_INF)
        m_own = jnp.max(so, axis=-1, keepdims=True)
        e_own = jnp.exp(so - m_own)
        l_own = jnp.sum(e_own, axis=-1, keepdims=True)
        acc_own = jnp.dot(e_own.astype(BF16), _pad_rows(_stack_heads(vn_ref[...])).astype(BF16),
                          preferred_element_type=F32)
        picks = [sel[:, n:n + 1] > 0.0 for n in range(n_blocks)]
        m_tot = m_own
        for n, pick in enumerate(picks):
            m_tot = jnp.maximum(m_tot, jnp.where(pick, m_all[n], NEG_INF))
        w_own = jnp.exp(m_own - m_tot)
        acc = w_own * acc_own
        l_tot = w_own * l_own
        for n, pick in enumerate(picks):
            w = jnp.where(pick, jnp.exp(m_all[n] - m_tot), 0.0)
            acc = acc + w * acc_all[n]
            l_tot = l_tot + w * l_all[n]
        o_ref[...] = _unstack_heads(acc / l_tot)


def _moba_sample(page_table, q, kn, vn, kc, vc, slot):
    n = q.shape[0]
    nb, npages = page_table.shape
    pages_per_block = MOBA_BLOCK // (kc.shape[2] // N_HEADS)
    n_blocks = npages // pages_per_block
    group = PAGE_GROUP
    assert n_blocks * N_HEADS <= LANES and group % pages_per_block == 0 and npages % group == 0

    def rows(w):
        return pl.BlockSpec((SUBLANES, w), lambda b, p, pt: (b, 0))

    pages = _page_specs(slot, group)
    grid_spec = pltpu.PrefetchScalarGridSpec(
        num_scalar_prefetch=1,
        grid=(nb, npages // group),
        in_specs=[rows(D_MODEL), rows(D_MODEL), rows(D_MODEL)] + pages + pages,
        out_specs=rows(D_MODEL),
        scratch_shapes=[pltpu.VMEM((HT_ROWS, HEAD_DIM), F32), pltpu.VMEM((HT_ROWS, HEAD_DIM), BF16),
                        pltpu.VMEM((n_blocks, HT_ROWS, 1), F32), pltpu.VMEM((n_blocks, HT_ROWS, 1), F32),
                        pltpu.VMEM((n_blocks, HT_ROWS, HEAD_DIM), F32),
                        pltpu.VMEM((LANES, HEAD_DIM), F32)])
    return pl.pallas_call(
        functools.partial(_moba_sample_kernel, group=group, pages_per_block=pages_per_block),
        grid_spec=grid_spec,
        out_shape=jax.ShapeDtypeStruct((n, D_MODEL), F32),
        compiler_params=_params("parallel", "arbitrary"),
        name="moba_sample",
    )(page_table, q, kn, vn, *([kc] * group), *([vc] * group))


def _pad_lanes(a, width=LANES):
    return jnp.pad(a, [(0, 0)] * (a.ndim - 1) + [(0, width - a.shape[-1])])


def _rope_tables(pos):
    half = HEAD_DIM // 2
    inv = ROPE_THETA ** (-jnp.arange(half, dtype=F32) / half)
    ang = pos.astype(F32)[:, None] * inv[None, :]
    cos, sin = jnp.cos(ang), jnp.sin(ang)
    return jnp.concatenate([cos, cos], axis=-1), jnp.concatenate([-sin, sin], axis=-1)


def kernel(x_prompt, x_sample, cache_fox_k, cache_fox_v, cache_fox_logf, cache_moba_k, cache_moba_v,
           state_mlstm_c, state_mlstm_n, state_mlstm_m, state_ffn_conv, page_table,
           norm_mix_g, norm_ffn_g, norm_final_g,
           mlstm_w_in, mlstm_b_gates, mlstm_norm_g, mlstm_w_out,
           fox_w_in, fox_b_f, fox_w_out, moba_w_in, moba_w_out,
           ffn_w_up, ffn_conv_w, ffn_conv_b, ffn_w_down):
    B, S, D = x_prompt.shape
    Bd, T, _ = x_sample.shape
    H, Dh, dk = N_HEADS, HEAD_DIM, MLSTM_QK
    depth = norm_mix_g.shape[0]
    n_pool, page_rows = cache_fox_k.shape[1], cache_fox_k.shape[2]
    past = page_table.shape[1] * page_rows
    assert T == SUBLANES and D == D_MODEL and page_rows == LANES

    xp = x_prompt.reshape(B * S, D)
    xs = x_sample.reshape(Bd * T, D)
    fkp, fks, fvp, fvs, flp, fls = [], [], [], [], [], []
    mkp, mks, mvp, mvs = [], [], [], []
    acp, acs, anp, ans, amp, ams = [], [], [], [], [], []
    cvp, cvs = [], []

    for i in range(depth):
        kind, slot = i % N_MIXERS, i // N_MIXERS
        g_mix = norm_mix_g[i]
        if kind == 0:
            w = mlstm_w_in[slot]
            wq = w[:, :H * dk] * (dk ** -0.5)
            wk = w[:, H * dk:2 * H * dk]
            rest = w[:, 2 * H * dk:2 * H * dk + 2 * H * Dh]
            wig = _pad_lanes(w[:, 2 * H * dk + 2 * H * Dh:2 * H * dk + 2 * H * Dh + H])
            wfg = _pad_lanes(w[:, 2 * H * dk + 2 * H * Dh + H:])
            pad_heads = lambda a: _pad_lanes(a.reshape(D, H, dk), Dh).reshape(D, H * Dh)
            w_p = jnp.concatenate([pad_heads(wq), pad_heads(wk), rest, wig, wfg], axis=1).astype(BF16)
            w_s = jnp.concatenate([wq, wk, rest, wig, wfg], axis=1).astype(BF16)
            bi = _pad_lanes(mlstm_b_gates[slot][:H].reshape(1, H))
            bfg = _pad_lanes(mlstm_b_gates[slot][H:].reshape(1, H))
            ng = mlstm_norm_g[slot].reshape(1, H * Dh)
            wo = mlstm_w_out[slot].astype(BF16)

            q, k, v, o, ig, fg = _norm_proj(xp, g_mix, w_p, [H * Dh, H * Dh, H * Dh, H * Dh, LANES, LANES])
            hp, c_, n_, m_ = _mlstm_prompt(q, k, v, o, ig, fg, bi, bfg, ng, B)
            acp.append(c_); anp.append(n_[:, :, :dk]); amp.append(m_.reshape(B, H))
            xp = _proj_res(hp, wo, xp)

            q, k, v, o, ig, fg = _norm_proj(xs, g_mix, w_s, [H * dk, H * dk, H * Dh, H * Dh, LANES, LANES])
            hs_, c_, n_, m_ = _mlstm_sample(q, k, v, o, ig, fg, bi, bfg, ng,
                                            state_mlstm_c[slot], state_mlstm_n[slot], state_mlstm_m[slot])
            acs.append(c_); ans.append(n_.reshape(Bd, H, dk)); ams.append(m_.reshape(Bd, H))
            xs = _proj_res(hs_, wo, xs)
        elif kind == 1:
            w = fox_w_in[slot]
            w_b = jnp.concatenate([w[:, :3 * H * Dh], _pad_lanes(w[:, 3 * H * Dh:])], axis=1).astype(BF16)
            bfg = _pad_lanes(fox_b_f[slot].reshape(1, H))
            wo = fox_w_out[slot].astype(BF16)
            widths = [H * Dh, H * Dh, H * Dh, LANES]

            q, k, v, fraw = _norm_proj(xp, g_mix, w_b, widths)
            lf, crow = _fox_gates(fraw, bfg, B)
            op = _fox_prompt(q, k, v, crow, B)
            fkp.append(k.reshape(B, S, H, Dh)); fvp.append(v.reshape(B, S, H, Dh))
            flp.append(lf[:, :H].reshape(B, S, H))
            xp = _proj_res(op, wo, xp)

            q, k, v, fraw = _norm_proj(xs, g_mix, w_b, widths)
            fflat = _pad_lanes(fraw[:, :H].reshape(Bd, T, H).transpose(0, 2, 1).reshape(Bd, 1, H * T))
            bflat = _pad_lanes(jnp.repeat(fox_b_f[slot], T).reshape(1, H * T))
            n_slots = cache_fox_k.shape[0]
            os_, lf = _fox_sample(page_table, q, k, v, fflat, bflat,
                                  cache_fox_logf[slot].astype(F32).reshape(n_pool, page_rows * H),
                                  cache_fox_k.astype(F32).reshape(n_slots, n_pool, page_rows * H, Dh),
                                  cache_fox_v.astype(F32).reshape(n_slots, n_pool, page_rows * H, Dh), slot)
            fks.append(k.reshape(Bd, T, H, Dh)); fvs.append(v.reshape(Bd, T, H, Dh))
            fls.append(lf[:, 0, :H * T].reshape(Bd, H, T).transpose(0, 2, 1))
            xs = _proj_res(os_, wo, xs)
        else:
            w_b = moba_w_in[slot].astype(BF16)
            wo = moba_w_out[slot].astype(BF16)
            widths = [H * Dh, H * Dh, H * Dh]
            cos_p, sin_p = _rope_tables(jnp.arange(S, dtype=jnp.int32))
            cos_s, sin_s = _rope_tables(past + jnp.arange(T, dtype=jnp.int32))
            reps = min(MOBA_BLOCK, Bd * T) // T
            cos_s, sin_s = jnp.tile(cos_s, (reps, 1)), jnp.tile(sin_s, (reps, 1))

            q, k, v = _norm_proj(xp, g_mix, w_b, widths)
            q, k, km = _rope(q, k, cos_p, sin_p)
            nblk = S // MOBA_BLOCK
            km = jnp.pad(km.reshape(B, nblk, D), ((0, 0), (0, -nblk % SUBLANES), (0, 0)))
            op = _moba_prompt(q, k, v, km, B)
            mkp.append(k.reshape(B, S, H, Dh)); mvp.append(v.reshape(B, S, H, Dh))
            xp = _proj_res(op, wo, xp)

            q, k, v = _norm_proj(xs, g_mix, w_b, widths)
            q, k, _ = _rope(q, k, cos_s, sin_s)
            n_slots = cache_moba_k.shape[0]
            os_ = _moba_sample(page_table, q, k, v,
                               cache_moba_k.astype(F32).reshape(n_slots, n_pool, page_rows * H, Dh),
                               cache_moba_v.astype(F32).reshape(n_slots, n_pool, page_rows * H, Dh), slot)
            mks.append(k.reshape(Bd, T, H, Dh)); mvs.append(v.reshape(Bd, T, H, Dh))
            xs = _proj_res(os_, wo, xs)

        wup = ffn_w_up[i].astype(BF16)
        wdn = ffn_w_down[i].astype(BF16)
        xp, tail = _ffn_prompt(xp, norm_ffn_g[i], wup, ffn_conv_w[i], ffn_conv_b[i], wdn, B)
        cvp.append(tail.reshape(B, SUBLANES, 2 * D_FF)[:, SUBLANES - (CONV_W - 1):])
        p2 = jnp.pad(state_ffn_conv[i], ((0, 0), (0, T - (CONV_W - 1)), (0, 0))).reshape(Bd * T, 2 * D_FF)
        xs, u = _ffn_sample(xs, norm_ffn_g[i], wup, ffn_conv_w[i], ffn_conv_b[i], wdn, p2)
        cvs.append(u.reshape(Bd, T, 2 * D_FF)[:, T - (CONV_W - 1):])

    y_prompt = _final_norm(xp, norm_final_g).reshape(B, S, D)
    y_sample = _final_norm(xs, norm_final_g).reshape(Bd, T, D)
    return (y_prompt, y_sample,
            jnp.stack(fkp), jnp.stack(fks), jnp.stack(fvp), jnp.stack(fvs), jnp.stack(flp), jnp.stack(fls),
            jnp.stack(mkp), jnp.stack(mks), jnp.stack(mvp), jnp.stack(mvs),
            jnp.stack(acp), jnp.stack(acs), jnp.stack(anp), jnp.stack(ans), jnp.stack(amp), jnp.stack(ams),
            jnp.stack(cvp), jnp.stack(cvs))
```

```python
import functools

import jax
import jax.numpy as jnp
from jax import lax
from jax.experimental import pallas as pl
from jax.experimental.pallas import tpu as pltpu

F32 = jnp.float32
BF16 = jnp.bfloat16
HI = lax.Precision.HIGHEST

D_MODEL = 1024
N_HEADS = 8
HEAD_DIM = 128
MLSTM_QK = 64
D_FF = 2816
CONV_W = 3
N_MIXERS = 3
MOBA_BLOCK = 256
MOBA_TOPK = 3
ROPE_THETA = 10000.0
RMS_EPS = 1e-6
NEG_INF = float("-inf")
M_FLOOR = -1e30
LANES = 128
SUBLANES = 8
VMEM_LIMIT = 56 * 1024 * 1024

NT_DIMS = (((1,), (1,)), ((), ()))
TN_DIMS = (((0,), (0,)), ((), ()))


def _params(*sem):
    return pltpu.CompilerParams(dimension_semantics=sem, vmem_limit_bytes=VMEM_LIMIT)


def _resident(shape):
    nd = len(shape)
    return pl.BlockSpec(shape, lambda *_: (0,) * nd, pipeline_mode=pl.Buffered(1))


def _rms(x, g):
    return x * lax.rsqrt(jnp.mean(x * x, axis=-1, keepdims=True) + RMS_EPS) * g


def _iota(shape, axis):
    return lax.broadcasted_iota(jnp.int32, shape, axis)


def _eye_rows(n):
    return (_iota((n, LANES), 0) == _iota((n, LANES), 1)).astype(F32)


def _nt(a, b, precision=None):
    return lax.dot_general(a, b, NT_DIMS, precision=precision, preferred_element_type=F32)


def _norm_proj_kernel(x_ref, g_ref, w_ref, *o_refs, widths):
    xn = _rms(x_ref[...], g_ref[...]).astype(BF16)
    off = 0
    for o_ref, wd in zip(o_refs, widths):
        for c in range(0, wd, 512):
            cw = min(512, wd - c)
            o_ref[:, c:c + cw] = jnp.dot(xn, w_ref[:, off + c:off + c + cw],
                                         preferred_element_type=F32)
        off += wd


def _norm_proj(x, g, w, widths, tm=256):
    n = x.shape[0]
    tm = min(tm, n)
    assert sum(widths) == w.shape[1] and n % tm == 0
    return pl.pallas_call(
        functools.partial(_norm_proj_kernel, widths=tuple(widths)),
        grid=(n // tm,),
        in_specs=[pl.BlockSpec((tm, D_MODEL), lambda i: (i, 0)),
                  _resident((1, D_MODEL)),
                  _resident(w.shape)],
        out_specs=[pl.BlockSpec((tm, wd), lambda i: (i, 0)) for wd in widths],
        out_shape=[jax.ShapeDtypeStruct((n, wd), F32) for wd in widths],
        compiler_params=_params("parallel"),
        name="norm_proj",
    )(x, g.reshape(1, D_MODEL), w)


def _proj_res_kernel(h_ref, w_ref, x_ref, o_ref):
    o_ref[...] = x_ref[...] + jnp.dot(h_ref[...].astype(BF16), w_ref[...],
                                      preferred_element_type=F32)


def _proj_res(h, w, x, tm=512):
    n = x.shape[0]
    tm = min(tm, n)
    return pl.pallas_call(
        _proj_res_kernel,
        grid=(n // tm,),
        in_specs=[pl.BlockSpec((tm, h.shape[1]), lambda i: (i, 0)),
                  _resident(w.shape),
                  pl.BlockSpec((tm, D_MODEL), lambda i: (i, 0))],
        out_specs=pl.BlockSpec((tm, D_MODEL), lambda i: (i, 0)),
        out_shape=jax.ShapeDtypeStruct((n, D_MODEL), F32),
        compiler_params=_params("parallel"),
        name="proj_res",
    )(h, w, x)


def _final_norm_kernel(x_ref, g_ref, o_ref):
    o_ref[...] = _rms(x_ref[...], g_ref[...])


def _final_norm(x, g, tm=512):
    n = x.shape[0]
    tm = min(tm, n)
    return pl.pallas_call(
        _final_norm_kernel,
        grid=(n // tm,),
        in_specs=[pl.BlockSpec((tm, D_MODEL), lambda i: (i, 0)), _resident((1, D_MODEL))],
        out_specs=pl.BlockSpec((tm, D_MODEL), lambda i: (i, 0)),
        out_shape=jax.ShapeDtypeStruct((n, D_MODEL), F32),
        compiler_params=_params("parallel"),
        name="final_norm",
    )(x, g.reshape(1, D_MODEL))


FFN_CH = 256


def _ffn_kernel(*refs, tt, grouped):
    if grouped:
        x_ref, g_ref, wup_ref, cw_ref, cb_ref, wdn_ref, p2_ref, o_ref, u_ref, hbuf = refs
    else:
        x_ref, g_ref, wup_ref, cw_ref, cb_ref, wdn_ref, o_ref, tail_ref, hbuf, carry = refs

        @pl.when(pl.program_id(1) == 0)
        def _():
            carry[...] = jnp.zeros_like(carry)

    x = x_ref[...]
    xn = _rms(x, g_ref[...]).astype(BF16)
    row = _iota((tt, 1), 0)
    for c in range(D_FF // FFN_CH):
        ys = []
        for part in range(2):
            c0 = part * D_FF + c * FFN_CH
            cols = slice(c0, c0 + FFN_CH)
            u = jnp.dot(xn, wup_ref[:, cols], preferred_element_type=F32)
            r1 = pltpu.roll(u, 1, 0)
            r2 = pltpu.roll(u, 2, 0)
            if grouped:
                p2 = p2_ref[:, cols]
                p1 = pltpu.roll(p2, tt - 1, 0)
                t8 = row & (SUBLANES - 1)
                um1 = jnp.where(t8 < 1, p1, r1)
                um2 = jnp.where(t8 < 2, p2, r2)
                u_ref[:, cols] = u
            else:
                pc = carry[:, cols]
                row8 = row[:SUBLANES]
                f1 = jnp.where(row8 < 1, pltpu.roll(pc, 1, 0), r1[:SUBLANES])
                f2 = jnp.where(row8 < 2, pltpu.roll(pc, 2, 0), r2[:SUBLANES])
                um1 = jnp.concatenate([f1, r1[SUBLANES:]], axis=0)
                um2 = jnp.concatenate([f2, r2[SUBLANES:]], axis=0)
                tail = u[tt - SUBLANES:]
                carry[:, cols] = tail
                tail_ref[:, cols] = tail
            y = cb_ref[:, cols] + cw_ref[0:1, cols] * um2
            y = y + cw_ref[1:2, cols] * um1
            y = y + cw_ref[2:3, cols] * u
            ys.append(y)
        gate, val = ys
        hbuf[:, c * FFN_CH:(c + 1) * FFN_CH] = (gate * jax.nn.sigmoid(gate) * val).astype(BF16)
    o_ref[...] = x + jnp.dot(hbuf[...], wdn_ref[...], preferred_element_type=F32)


def _ffn_prompt(x, g, wup, cw, cb, wdn, nseq, tt=512):
    n = x.shape[0]
    nt = n // nseq // tt
    return pl.pallas_call(
        functools.partial(_ffn_kernel, tt=tt, grouped=False),
        grid=(nseq, nt),
        in_specs=[pl.BlockSpec((tt, D_MODEL), lambda b, t: (b * nt + t, 0)),
                  _resident((1, D_MODEL)), _resident(wup.shape), _resident(cw.shape),
                  _resident((1, 2 * D_FF)), _resident(wdn.shape)],
        out_specs=[pl.BlockSpec((tt, D_MODEL), lambda b, t: (b * nt + t, 0)),
                   pl.BlockSpec((SUBLANES, 2 * D_FF), lambda b, t: (b, 0))],
        out_shape=[jax.ShapeDtypeStruct((n, D_MODEL), F32),
                   jax.ShapeDtypeStruct((nseq * SUBLANES, 2 * D_FF), F32)],
        scratch_shapes=[pltpu.VMEM((tt, D_FF), BF16), pltpu.VMEM((SUBLANES, 2 * D_FF), F32)],
        compiler_params=_params("parallel", "arbitrary"),
        name="ffn_prompt",
    )(x, g.reshape(1, D_MODEL), wup, cw, cb.reshape(1, 2 * D_FF), wdn)


def _ffn_sample(x, g, wup, cw, cb, wdn, p2, tt=128):
    n = x.shape[0]
    tt = min(tt, n)
    return pl.pallas_call(
        functools.partial(_ffn_kernel, tt=tt, grouped=True),
        grid=(n // tt,),
        in_specs=[pl.BlockSpec((tt, D_MODEL), lambda i: (i, 0)),
                  _resident((1, D_MODEL)), _resident(wup.shape), _resident(cw.shape),
                  _resident((1, 2 * D_FF)), _resident(wdn.shape),
                  pl.BlockSpec((tt, 2 * D_FF), lambda i: (i, 0))],
        out_specs=[pl.BlockSpec((tt, D_MODEL), lambda i: (i, 0)),
                   pl.BlockSpec((tt, 2 * D_FF), lambda i: (i, 0))],
        out_shape=[jax.ShapeDtypeStruct((n, D_MODEL), F32),
                   jax.ShapeDtypeStruct((n, 2 * D_FF), F32)],
        scratch_shapes=[pltpu.VMEM((tt, D_FF), BF16)],
        compiler_params=_params("parallel"),
        name="ffn_sample",
    )(x, g.reshape(1, D_MODEL), wup, cw, cb.reshape(1, 2 * D_FF), wdn, p2)


def _mlstm_prompt_kernel(q_ref, k_ref, v_ref, o_ref, ig_ref, fg_ref, bi_ref, bf_ref, ng_ref,
                         h_ref, cout_ref, nout_ref, mout_ref, c_s, m_s, *, chunk):
    L = chunk
    j = pl.program_id(1)

    def tile(x, n):
        return jnp.concatenate([x] * n, axis=1)

    @pl.when(j == 0)
    def _():
        c_s[...] = jnp.zeros_like(c_s)
        m_s[...] = jnp.zeros_like(m_s)

    ig = ig_ref[...] + bi_ref[...]
    lf = jax.nn.log_sigmoid(fg_ref[...] + bf_ref[...])
    tril = (_iota((L, L), 0) >= _iota((L, L), 1))
    bt = jnp.dot(tril.astype(F32), lf, precision=HI, preferred_element_type=F32)
    rc = ig - bt
    r_t = _nt(_eye_rows(SUBLANES), rc, HI)
    mean_mat = jnp.full((HEAD_DIM, HEAD_DIM), 1.0 / HEAD_DIM, F32)
    ones_b = jnp.ones((L, LANES), BF16)
    for h in range(N_HEADS):
        hs = slice(h * HEAD_DIM, (h + 1) * HEAD_DIM)
        kh = k_ref[:, hs]
        qb, kb = q_ref[:, hs].astype(BF16), kh.astype(BF16)
        v_ones = jnp.concatenate([v_ref[:, hs].astype(BF16), ones_b], axis=1)
        bcol = jnp.broadcast_to(bt[:, h:h + 1], (L, LANES))
        rcol = jnp.broadcast_to(rc[:, h:h + 1], (L, LANES))
        dm = jnp.where(tril, tile(bcol, L // LANES) + r_t[h:h + 1, :], NEG_INF)
        m_prev = m_s[h]
        inter = bcol + m_prev
        mt = jnp.maximum(jnp.max(dm, axis=-1, keepdims=True), inter)
        w_intra = jnp.exp(dm - tile(mt, L // LANES))
        w_inter = jnp.exp(inter - mt)
        c_prev = c_s[h]
        s = _nt(qb, kb) * w_intra
        tot = (jnp.dot(s.astype(BF16), v_ones, preferred_element_type=F32)
               + jnp.dot(qb, c_prev.astype(BF16), preferred_element_type=F32) * tile(w_inter, 2))
        den = jnp.maximum(jnp.abs(tot[:, HEAD_DIM:]), jnp.exp(-mt))
        hh = tot[:, :HEAD_DIM] / den
        ms = jnp.dot(hh * hh, mean_mat, precision=HI, preferred_element_type=F32)
        hh = hh * lax.rsqrt(ms + RMS_EPS) * ng_ref[:, hs]
        h_ref[:, hs] = hh * jax.nn.sigmoid(o_ref[:, hs])
        b_last = bcol[L - 1:L, :]
        gcol = b_last + rcol
        m_new = jnp.maximum(b_last + m_prev, jnp.max(gcol, axis=0, keepdims=True))
        a_prev = jnp.exp(b_last + m_prev - m_new)
        ka = kh * jnp.exp(gcol - m_new)
        c_s[h] = tile(a_prev, 2) * c_prev + lax.dot_general(ka.astype(BF16), v_ones, TN_DIMS,
                                                            preferred_element_type=F32)
        m_s[h] = m_new

    @pl.when(j == pl.num_programs(1) - 1)
    def _():
        for h in range(N_HEADS):
            cout_ref[0, h] = c_s[h][:MLSTM_QK, :HEAD_DIM]
            nout_ref[0, h:h + 1, :] = c_s[h][:, HEAD_DIM:].T[0:1, :]
            mout_ref[0, :, h:h + 1] = m_s[h][:, 0:1]


def _mlstm_prompt(q, k, v, o, ig, fg, bi, bfg, ng, nseq, chunk=256):
    n = q.shape[0]
    nc = n // nseq // chunk
    wide = pl.BlockSpec((chunk, D_MODEL), lambda b, j: (b * nc + j, 0))
    narrow = pl.BlockSpec((chunk, LANES), lambda b, j: (b * nc + j, 0))
    return pl.pallas_call(
        functools.partial(_mlstm_prompt_kernel, chunk=chunk),
        grid=(nseq, nc),
        in_specs=[wide, wide, wide, wide, narrow, narrow,
                  _resident((1, LANES)), _resident((1, LANES)), _resident((1, D_MODEL))],
        out_specs=[wide,
                   pl.BlockSpec((1, N_HEADS, MLSTM_QK, HEAD_DIM), lambda b, j: (b, 0, 0, 0)),
                   pl.BlockSpec((1, N_HEADS, LANES), lambda b, j: (b, 0, 0)),
                   pl.BlockSpec((1, 1, N_HEADS), lambda b, j: (b, 0, 0))],
        out_shape=[jax.ShapeDtypeStruct((n, D_MODEL), F32),
                   jax.ShapeDtypeStruct((nseq, N_HEADS, MLSTM_QK, HEAD_DIM), F32),
                   jax.ShapeDtypeStruct((nseq, N_HEADS, LANES), F32),
                   jax.ShapeDtypeStruct((nseq, 1, N_HEADS), F32)],
        scratch_shapes=[pltpu.VMEM((N_HEADS, HEAD_DIM, 2 * HEAD_DIM), F32),
                        pltpu.VMEM((N_HEADS, 1, LANES), F32)],
        compiler_params=_params("parallel", "arbitrary"),
        name="mlstm_prompt",
    )(q, k, v, o, ig, fg, bi, bfg, ng)


def _cumsum_rows8(x):
    row = _iota(x.shape, 0)
    for sh in (1, 2, 4):
        x = x + jnp.where(row >= sh, pltpu.roll(x, sh, 0), 0.0)
    return x


def _stack_cols(x, n=N_HEADS):
    return jnp.concatenate([x[:, h:h + 1] for h in range(n)], axis=0)


def _stack_bcast(x, rows, n=N_HEADS):
    return jnp.concatenate([jnp.broadcast_to(x[:, h:h + 1], (rows, 1)) for h in range(n)], axis=0)


def _pad_rows(x, rows=LANES):
    return jnp.concatenate([x, jnp.zeros((rows - x.shape[0], x.shape[1]), x.dtype)], axis=0)


def _mlstm_sample_kernel(q_ref, k_ref, v_ref, o_ref, ig_ref, fg_ref, bi_ref, bf_ref, ng_ref,
                         c0_ref, n0_ref, m0_ref, h_ref, cout_ref, nout_ref, mout_ref):
    T, HT, QW = SUBLANES, N_HEADS * SUBLANES, N_HEADS * MLSTM_QK
    ig = ig_ref[...] + bi_ref[...]
    lf = jax.nn.log_sigmoid(fg_ref[...] + bf_ref[...])
    bt = _cumsum_rows8(lf)
    rc = ig - bt
    m0 = m0_ref[0]
    bcol = _stack_cols(bt)
    r_t = _nt(_eye_rows(SUBLANES), _pad_rows(rc), HI)
    rrow = jnp.concatenate([jnp.broadcast_to(r_t[h:h + 1, :], (T, LANES))
                            for h in range(N_HEADS)], axis=0)
    lane = _iota((HT, LANES), 1)
    tok = _iota((HT, LANES), 0) & (T - 1)
    dm = jnp.where(lane <= tok, bcol + rrow, NEG_INF)
    inter = bcol + _stack_bcast(m0, T)
    mt = jnp.maximum(jnp.max(dm, axis=-1, keepdims=True), inter)
    w_intra = jnp.exp(dm - mt)
    w_inter = jnp.exp(inter - mt)

    head_of_row = _iota((HT, QW), 0) // T
    head_of_lane = _iota((HT, QW), 1) // MLSTM_QK
    diag = head_of_row == head_of_lane
    q8, k8, v8 = q_ref[...], k_ref[...], v_ref[...]
    qbd = jnp.where(diag, jnp.concatenate([q8] * N_HEADS, axis=0), 0.0)
    qbd_b = qbd.astype(BF16)
    s = _nt(qbd_b, _pad_rows(k8).astype(BF16)) * w_intra
    nv = jnp.dot(s.astype(BF16), _pad_rows(v8).astype(BF16), preferred_element_type=F32)
    row_head = _iota((HT, HEAD_DIM), 0) // T
    num = jnp.zeros((HT, HEAD_DIM), F32)
    for h in range(N_HEADS):
        num = num + jnp.where(row_head == h, nv[:, h * HEAD_DIM:(h + 1) * HEAD_DIM], 0.0)
    c_prev = c0_ref[0].reshape(QW, HEAD_DIM)
    n_prev = n0_ref[0]
    num = num + jnp.dot(qbd_b, c_prev.astype(BF16), preferred_element_type=F32) * w_inter
    den = (jnp.sum(s, axis=-1, keepdims=True)
           + jnp.sum(qbd * n_prev, axis=-1, keepdims=True) * w_inter)
    den = jnp.maximum(jnp.abs(den), jnp.exp(-mt))
    hh = num / den
    ng = jnp.concatenate([jnp.broadcast_to(ng_ref[:, h * HEAD_DIM:(h + 1) * HEAD_DIM], (T, HEAD_DIM))
                          for h in range(N_HEADS)], axis=0)
    o8 = o_ref[...]
    ost = jnp.concatenate([o8[:, h * HEAD_DIM:(h + 1) * HEAD_DIM] for h in range(N_HEADS)], axis=0)
    hh = hh * lax.rsqrt(jnp.mean(hh * hh, axis=-1, keepdims=True) + RMS_EPS) * ng
    hh = hh * jax.nn.sigmoid(ost)
    h_ref[...] = jnp.concatenate([hh[h * T:(h + 1) * T, :] for h in range(N_HEADS)], axis=1)

    b_last = bt[T - 1:T, :]
    g2 = b_last + rc
    m_new = jnp.maximum(b_last + m0, jnp.max(g2, axis=0, keepdims=True))
    a_prev = jnp.exp(b_last + m0 - m_new)
    a_tok = jnp.exp(g2 - m_new)
    kabd = jnp.where(diag, jnp.concatenate([k8] * N_HEADS, axis=0) * _stack_cols(a_tok), 0.0)
    vst = jnp.concatenate([v8[:, h * HEAD_DIM:(h + 1) * HEAD_DIM] for h in range(N_HEADS)], axis=0)
    dc = lax.dot_general(kabd.astype(BF16), vst.astype(BF16), TN_DIMS, preferred_element_type=F32)
    c_new = _stack_bcast(a_prev, MLSTM_QK) * c_prev + dc
    cout_ref[0] = c_new.reshape(N_HEADS, MLSTM_QK, HEAD_DIM)
    a_lane = jnp.max(jnp.where(diag, _stack_bcast(a_prev, T), 0.0), axis=0, keepdims=True)
    nout_ref[0] = a_lane * n_prev + jnp.sum(kabd, axis=0, keepdims=True)
    mout_ref[0] = m_new[:, :N_HEADS]


def _mlstm_sample(q, k, v, o, ig, fg, bi, bfg, ng, c0, n0, m0):
    n = q.shape[0]
    nb = n // SUBLANES
    qw = N_HEADS * MLSTM_QK

    def rows(w):
        return pl.BlockSpec((SUBLANES, w), lambda b: (b, 0))

    cspec = pl.BlockSpec((1, N_HEADS, MLSTM_QK, HEAD_DIM), lambda b: (b, 0, 0, 0))
    nspec = pl.BlockSpec((1, 1, qw), lambda b: (b, 0, 0))
    mspec = pl.BlockSpec((1, 1, N_HEADS), lambda b: (b, 0, 0))
    return pl.pallas_call(
        _mlstm_sample_kernel,
        grid=(nb,),
        in_specs=[rows(qw), rows(qw), rows(D_MODEL), rows(D_MODEL), rows(LANES), rows(LANES),
                  _resident((1, LANES)), _resident((1, LANES)), _resident((1, D_MODEL)),
                  cspec, nspec, pl.BlockSpec((1, 1, LANES), lambda b: (b, 0, 0))],
        out_specs=[rows(D_MODEL), cspec, nspec, mspec],
        out_shape=[jax.ShapeDtypeStruct((n, D_MODEL), F32),
                   jax.ShapeDtypeStruct((nb, N_HEADS, MLSTM_QK, HEAD_DIM), F32),
                   jax.ShapeDtypeStruct((nb, 1, qw), F32),
                   jax.ShapeDtypeStruct((nb, 1, N_HEADS), F32)],
        compiler_params=_params("parallel"),
        name="mlstm_sample",
    )(q, k, v, o, ig, fg, bi, bfg, ng, c0, n0.reshape(nb, 1, qw), _pad_lanes(m0).reshape(nb, 1, LANES))


def _fox_gate_kernel(f_ref, bf_ref, lf_ref, crow_ref, carry, *, ts):
    @pl.when(pl.program_id(1) == 0)
    def _():
        carry[...] = jnp.zeros_like(carry)

    lf = jax.nn.log_sigmoid(f_ref[...] + bf_ref[...])
    tril = (_iota((ts, ts), 0) >= _iota((ts, ts), 1)).astype(F32)
    c = jnp.dot(tril, lf, precision=HI, preferred_element_type=F32) + carry[...]
    carry[...] = c[ts - 1:ts, :]
    lf_ref[...] = lf
    crow_ref[0] = _nt(_eye_rows(SUBLANES), c, HI)


def _fox_gates(fraw, bfg, nseq, ts=512):
    n = fraw.shape[0]
    seq = n // nseq
    nt = seq // ts
    tok = pl.BlockSpec((ts, LANES), lambda b, t: (b * nt + t, 0))
    return pl.pallas_call(
        functools.partial(_fox_gate_kernel, ts=ts),
        grid=(nseq, nt),
        in_specs=[tok, _resident((1, LANES))],
        out_specs=[tok, pl.BlockSpec((1, SUBLANES, ts), lambda b, t: (b, 0, t))],
        out_shape=[jax.ShapeDtypeStruct((n, LANES), F32),
                   jax.ShapeDtypeStruct((nseq, SUBLANES, seq), F32)],
        scratch_shapes=[pltpu.VMEM((1, LANES), F32)],
        compiler_params=_params("parallel", "arbitrary"),
        name="fox_gates",
    )(fraw, bfg)


def _online_update(s, vb, m_ref, l_ref, acc_ref, hs):
    s_tiles, v_tiles = (s, vb) if isinstance(s, (list, tuple)) else ([s], [vb])
    m_prev = m_ref[...]
    m_new = m_prev
    for st in s_tiles:
        m_new = jnp.maximum(m_new, jnp.max(st, axis=-1, keepdims=True))
    alpha = jnp.exp(m_prev - m_new)
    l_new = alpha * l_ref[...]
    acc = alpha * acc_ref[:, hs]
    for st, vt in zip(s_tiles, v_tiles):
        p = jnp.exp(st - m_new)
        l_new = l_new + jnp.sum(p, axis=-1, keepdims=True)
        acc = acc + jnp.dot(p.astype(BF16), vt, preferred_element_type=F32)
    l_ref[...] = l_new
    acc_ref[:, hs] = acc
    m_ref[...] = m_new


def _flash_update(s, v, m_ref, l_ref, acc_ref, hs):
    m_prev = m_ref[...]
    m_new = jnp.maximum(m_prev, jnp.max(s, axis=-1, keepdims=True))
    alpha = jnp.exp(m_prev - m_new)
    p = jnp.exp(s - jnp.concatenate([m_new] * (s.shape[1] // LANES), axis=1))
    v_ones = jnp.concatenate([v.astype(BF16), jnp.ones((v.shape[0], LANES), BF16)], axis=1)
    pv = jnp.dot(p.astype(BF16), v_ones, preferred_element_type=F32)
    acc_ref[:, hs] = alpha * acc_ref[:, hs] + pv[:, :HEAD_DIM]
    l_ref[...] = alpha * l_ref[...] + pv[:, HEAD_DIM:]
    m_ref[...] = m_new


def _fox_prompt_kernel(q_ref, k_ref, v_ref, crow_ref, o_ref, qs_s, m_s, l_s, acc_s, *, tq):
    i, j = pl.program_id(1), pl.program_id(2)

    @pl.when(j == 0)
    def _():
        qs_s[...] = (q_ref[...] * (HEAD_DIM ** -0.5)).astype(BF16)
        m_s[...] = jnp.full_like(m_s, M_FLOOR)
        l_s[...] = jnp.zeros_like(l_s)
        acc_s[...] = jnp.zeros_like(acc_s)

    def attend(diagonal):
        causal = _iota((tq, tq), 1) <= _iota((tq, tq), 0)
        for h in range(N_HEADS):
            hs = slice(h * HEAD_DIM, (h + 1) * HEAD_DIM)
            s = _nt(qs_s[:, hs], k_ref[:, hs].astype(BF16)) - crow_ref[0, h:h + 1, :]
            if diagonal:
                s = jnp.where(causal, s, NEG_INF)
            _flash_update(s, v_ref[:, hs], m_s.at[h], l_s.at[h], acc_s, hs)

    @pl.when(j < i)
    def _():
        attend(False)

    @pl.when(j == i)
    def _():
        attend(True)
        for h in range(N_HEADS):
            hs = slice(h * HEAD_DIM, (h + 1) * HEAD_DIM)
            o_ref[:, hs] = acc_s[:, hs] / l_s[h]


def _fox_prompt(q, k, v, crow, nseq, tq=512):
    n = q.shape[0]
    nq = n // nseq // tq
    qspec = pl.BlockSpec((tq, D_MODEL), lambda b, i, j: (b * nq + i, 0))
    kspec = pl.BlockSpec((tq, D_MODEL), lambda b, i, j: (b * nq + jnp.minimum(j, i), 0))
    return pl.pallas_call(
        functools.partial(_fox_prompt_kernel, tq=tq),
        grid=(nseq, nq, nq),
        in_specs=[qspec, kspec, kspec,
                  pl.BlockSpec((1, SUBLANES, tq), lambda b, i, j: (b, 0, jnp.minimum(j, i)))],
        out_specs=qspec,
        out_shape=jax.ShapeDtypeStruct((n, D_MODEL), F32),
        scratch_shapes=[pltpu.VMEM((tq, D_MODEL), BF16),
                        pltpu.VMEM((N_HEADS, tq, LANES), F32), pltpu.VMEM((N_HEADS, tq, LANES), F32),
                        pltpu.VMEM((tq, D_MODEL), F32)],
        compiler_params=_params("parallel", "parallel", "arbitrary"),
        name="fox_prompt",
    )(q, k, v, crow)


HT_ROWS = N_HEADS * SUBLANES
PAGE_ROWS = LANES * N_HEADS


def _stack_heads(x8):
    return jnp.concatenate([x8[:, h * HEAD_DIM:(h + 1) * HEAD_DIM] for h in range(N_HEADS)], axis=0)


def _unstack_heads(x):
    return jnp.concatenate([x[h * SUBLANES:(h + 1) * SUBLANES, :] for h in range(N_HEADS)], axis=1)


def _same_head_past():
    return ((_iota((HT_ROWS, PAGE_ROWS), 1) & (N_HEADS - 1))
            == (_iota((HT_ROWS, PAGE_ROWS), 0) // SUBLANES))


def _visible_new():
    row, lane = _iota((HT_ROWS, LANES), 0), _iota((HT_ROWS, LANES), 1)
    return (((lane // SUBLANES) == (row // SUBLANES))
            & ((lane & (SUBLANES - 1)) <= (row & (SUBLANES - 1))))


def _cumsum_keys(x):
    npg, w = x.shape
    lane, row = _iota(x.shape, 1), _iota(x.shape, 0)
    sh = N_HEADS
    while sh < w:
        r = pltpu.roll(x, sh, 1)
        from_prev_page = jnp.where(row >= 1, pltpu.roll(r, 1, 0), 0.0)
        x = x + jnp.where(lane >= sh, r, from_prev_page)
        sh *= 2
    sh = 1
    while sh < npg:
        x = x + jnp.where(row >= sh, pltpu.roll(x, sh, 0), 0.0)
        sh *= 2
    return x


def _fox_sample_kernel(pt_ref, q_ref, kn_ref, vn_ref, f_ref, bf_ref, lftab_ref, *rest, group):
    kc_refs, vc_refs = rest[:group], rest[group:2 * group]
    o_ref, lf_ref, q_s, m_s, l_s, acc_s, c_s = rest[2 * group:]
    b, p = pl.program_id(0), pl.program_id(1)
    npages = c_s.shape[0]
    one_head = slice(0, HEAD_DIM)

    @pl.when(p == 0)
    def _():
        q_s[...] = (_stack_heads(q_ref[...]) * (HEAD_DIM ** -0.5)).astype(BF16)
        m_s[...] = jnp.full_like(m_s, M_FLOOR)
        l_s[...] = jnp.zeros_like(l_s)
        acc_s[...] = jnp.zeros_like(acc_s)
        for pg in range(npages):
            c_s[pg:pg + 1, :] = lftab_ref[pl.ds(pt_ref[b, pg], 1), :]
        c_s[...] = _cumsum_keys(c_s[...])

    same_head = _same_head_past()
    q = q_s[...]
    s_tiles = [jnp.where(same_head,
                         _nt(q, kc_refs[g][0, 0].astype(BF16)) - c_s[pl.ds(p * group + g, 1), :],
                         NEG_INF) for g in range(group)]
    _online_update(s_tiles, [vc_refs[g][0, 0].astype(BF16) for g in range(group)],
                   m_s, l_s, acc_s, one_head)

    @pl.when(p == pl.num_programs(1) - 1)
    def _():
        lane = _iota((1, LANES), 1)
        lf = jax.nn.log_sigmoid(f_ref[0] + bf_ref[...])
        lf_ref[0] = lf
        cum = lf
        for sh in (1, 2, 4):
            cum = cum + jnp.where((lane & (SUBLANES - 1)) >= sh, pltpu.roll(cum, sh, 1), 0.0)
        tail = c_s[npages - 1:npages, PAGE_ROWS - LANES:]
        spread = ((_iota((LANES, LANES), 0) - (LANES - N_HEADS))
                  == (_iota((LANES, LANES), 1) // SUBLANES)).astype(F32)
        past = jnp.dot(jnp.broadcast_to(tail, (SUBLANES, LANES)), spread, precision=HI,
                       preferred_element_type=F32)[0:1]
        sn = _nt(q_s[...], _pad_rows(_stack_heads(kn_ref[...])).astype(BF16)) - (past + cum)
        sn = jnp.where(_visible_new(), sn, NEG_INF)
        _online_update(sn, _pad_rows(_stack_heads(vn_ref[...])).astype(BF16), m_s, l_s, acc_s, one_head)
        o_ref[...] = _unstack_heads(acc_s[...] / l_s[...])


PAGE_GROUP = 8


def _page_specs(slot, group):
    return [pl.BlockSpec((1, 1, PAGE_ROWS, HEAD_DIM),
                         lambda b, p, pt, g=g: (slot, pt[b, p * group + g], 0, 0))
            for g in range(group)]


def _fox_sample(page_table, q, kn, vn, fflat, bflat, lftab, kc, vc, slot):
    n = q.shape[0]
    nb, npages = page_table.shape
    group = PAGE_GROUP
    assert npages % group == 0

    def rows(w):
        return pl.BlockSpec((SUBLANES, w), lambda b, p, pt: (b, 0))

    pages = _page_specs(slot, group)
    flat = pl.BlockSpec((1, 1, LANES), lambda b, p, pt: (b, 0, 0))
    grid_spec = pltpu.PrefetchScalarGridSpec(
        num_scalar_prefetch=1,
        grid=(nb, npages // group),
        in_specs=[rows(D_MODEL), rows(D_MODEL), rows(D_MODEL), flat,
                  _resident((1, LANES)), _resident(lftab.shape)] + pages + pages,
        out_specs=[rows(D_MODEL), flat],
        scratch_shapes=[pltpu.VMEM((HT_ROWS, HEAD_DIM), BF16),
                        pltpu.VMEM((HT_ROWS, 1), F32), pltpu.VMEM((HT_ROWS, 1), F32),
                        pltpu.VMEM((HT_ROWS, HEAD_DIM), F32),
                        pltpu.VMEM((npages, PAGE_ROWS), F32)])
    return pl.pallas_call(
        functools.partial(_fox_sample_kernel, group=group),
        grid_spec=grid_spec,
        out_shape=[jax.ShapeDtypeStruct((n, D_MODEL), F32), jax.ShapeDtypeStruct((nb, 1, LANES), F32)],
        compiler_params=_params("parallel", "arbitrary"),
        name="fox_sample",
    )(page_table, q, kn, vn, fflat, bflat, lftab, *([kc] * group), *([vc] * group))


def _rope_kernel(q_ref, k_ref, cos_ref, sin_ref, qo_ref, ko_ref, km_ref):
    cos, sin = cos_ref[...], sin_ref[...]
    for h in range(N_HEADS):
        hs = slice(h * HEAD_DIM, (h + 1) * HEAD_DIM)
        for src, dst in ((q_ref, qo_ref), (k_ref, ko_ref)):
            x = src[:, hs]
            dst[:, hs] = x * cos + pltpu.roll(x, HEAD_DIM // 2, 1) * sin
    km_ref[0] = jnp.mean(ko_ref[...], axis=0, keepdims=True)


def _rope(q, k, cos, sin, tt=MOBA_BLOCK):
    n = q.shape[0]
    tt = min(tt, n)
    ntab = cos.shape[0] // tt
    tok = pl.BlockSpec((tt, D_MODEL), lambda i: (i, 0))
    tab = pl.BlockSpec((tt, HEAD_DIM), lambda i: (i % ntab, 0))
    return pl.pallas_call(
        _rope_kernel,
        grid=(n // tt,),
        in_specs=[tok, tok, tab, tab],
        out_specs=[tok, tok, pl.BlockSpec((1, 1, D_MODEL), lambda i: (i, 0, 0))],
        out_shape=[jax.ShapeDtypeStruct((n, D_MODEL), F32), jax.ShapeDtypeStruct((n, D_MODEL), F32),
                   jax.ShapeDtypeStruct((n // tt, 1, D_MODEL), F32)],
        compiler_params=_params("parallel"),
        name="rope",
    )(q, k, cos, sin)


def _top_blocks(gate, n_valid):
    lane = _iota(gate.shape, 1)
    g = jnp.where(lane < n_valid, gate, NEG_INF)
    sel = jnp.zeros(gate.shape, F32)
    for _ in range(MOBA_TOPK):
        mx = jnp.max(g, axis=-1, keepdims=True)
        idx = jnp.min(jnp.where(g == mx, lane, LANES), axis=-1, keepdims=True)
        hit = lane == idx
        sel = jnp.where(hit & (mx > NEG_INF), 1.0, sel)
        g = jnp.where(hit, NEG_INF, g)
    return sel


def _top_blocks_t(gate_t, n_valid):
    nb = gate_t.shape[0]
    blk = _iota(gate_t.shape, 0)
    g = jnp.where(blk < n_valid, gate_t, NEG_INF)
    sel = jnp.zeros(gate_t.shape, F32)
    for _ in range(MOBA_TOPK):
        mx = jnp.max(g, axis=0, keepdims=True)
        idx = jnp.min(jnp.where(g == mx, blk, nb), axis=0, keepdims=True)
        hit = blk == idx
        sel = jnp.where(hit & (mx > NEG_INF), 1.0, sel)
        g = jnp.where(hit, NEG_INF, g)
    return sel


MOBA_TILE = 2 * MOBA_BLOCK


def _moba_prompt_kernel(q_ref, k_ref, v_ref, km_ref, o_ref, qs_s, sel_s, m_s, l_s, acc_s):
    i, j = pl.program_id(1), pl.program_id(2)
    tq = MOBA_TILE
    blocks_per_tile = tq // MOBA_BLOCK

    @pl.when(j == 0)
    def _():
        qs_s[...] = (q_ref[...] * (HEAD_DIM ** -0.5)).astype(BF16)
        m_s[...] = jnp.full_like(m_s, M_FLOOR)
        l_s[...] = jnp.zeros_like(l_s)
        acc_s[...] = jnp.zeros_like(acc_s)
        own = i * blocks_per_tile + _iota((1, tq), 1) // MOBA_BLOCK
        for h in range(N_HEADS):
            hs = slice(h * HEAD_DIM, (h + 1) * HEAD_DIM)
            sel_t = _top_blocks_t(_nt(km_ref[0, :, hs], q_ref[:, hs], HI), own)
            sel_s[h] = _pad_rows(sel_t).T.astype(BF16)

    def attend(diagonal):
        key_block = j * blocks_per_tile + _iota((LANES, tq), 1) // MOBA_BLOCK
        spread = (_iota((LANES, tq), 0) == key_block).astype(BF16)
        if diagonal:
            row, col = _iota((tq, tq), 0), _iota((tq, tq), 1)
            own_causal = ((row // MOBA_BLOCK) == (col // MOBA_BLOCK)) & (col <= row)
        for h in range(N_HEADS):
            hs = slice(h * HEAD_DIM, (h + 1) * HEAD_DIM)
            visible = jnp.dot(sel_s[h], spread, preferred_element_type=F32) > 0.5
            if diagonal:
                visible = visible | own_causal
            s = jnp.where(visible, _nt(qs_s[:, hs], k_ref[:, hs].astype(BF16)), NEG_INF)
            _flash_update(s, v_ref[:, hs], m_s.at[h], l_s.at[h], acc_s, hs)

    @pl.when(j < i)
    def _():
        attend(False)

    @pl.when(j == i)
    def _():
        attend(True)
        for h in range(N_HEADS):
            hs = slice(h * HEAD_DIM, (h + 1) * HEAD_DIM)
            o_ref[:, hs] = acc_s[:, hs] / l_s[h]


def _moba_prompt(q, k, v, kmean, nseq):
    n = q.shape[0]
    tq = MOBA_TILE
    nq = n // nseq // tq
    assert kmean.shape[1] % SUBLANES == 0 and kmean.shape[1] <= LANES
    qspec = pl.BlockSpec((tq, D_MODEL), lambda b, i, j: (b * nq + i, 0))
    kspec = pl.BlockSpec((tq, D_MODEL), lambda b, i, j: (b * nq + jnp.minimum(j, i), 0))
    return pl.pallas_call(
        _moba_prompt_kernel,
        grid=(nseq, nq, nq),
        in_specs=[qspec, kspec, kspec,
                  pl.BlockSpec((1, kmean.shape[1], D_MODEL), lambda b, i, j: (b, 0, 0))],
        out_specs=qspec,
        out_shape=jax.ShapeDtypeStruct((n, D_MODEL), F32),
        scratch_shapes=[pltpu.VMEM((tq, D_MODEL), BF16), pltpu.VMEM((N_HEADS, tq, LANES), BF16),
                        pltpu.VMEM((N_HEADS, tq, LANES), F32), pltpu.VMEM((N_HEADS, tq, LANES), F32),
                        pltpu.VMEM((tq, D_MODEL), F32)],
        compiler_params=_params("parallel", "parallel", "arbitrary"),
        name="moba_prompt",
    )(q, k, v, kmean)


def _moba_sample_kernel(pt_ref, q_ref, kn_ref, vn_ref, *rest, group, pages_per_block):
    kc_refs, vc_refs = rest[:group], rest[group:2 * group]
    o_ref, qf_s, qb_s, m_all, l_all, acc_all, ksum_s = rest[2 * group:]
    p = pl.program_id(1)
    n_blocks = acc_all.shape[0]
    blocks_per_step = group // pages_per_block

    @pl.when(p == 0)
    def _():
        q = _stack_heads(q_ref[...])
        qf_s[...] = q
        qb_s[...] = (q * (HEAD_DIM ** -0.5)).astype(BF16)
        ksum_s[...] = jnp.zeros_like(ksum_s)

    same_head = _same_head_past()
    q = qb_s[...]
    for bi in range(blocks_per_step):
        blk = p * blocks_per_step + bi
        ksum = jnp.zeros((N_HEADS, HEAD_DIM), F32)
        s_tiles, v_tiles = [], []
        for g in range(bi * pages_per_block, (bi + 1) * pages_per_block):
            kpage = kc_refs[g][0, 0]
            ksum = ksum + jnp.sum(kpage.reshape(LANES, N_HEADS, HEAD_DIM), axis=0)
            s_tiles.append(jnp.where(same_head, _nt(q, kpage.astype(BF16)), NEG_INF))
            v_tiles.append(vc_refs[g][0, 0].astype(BF16))
        ksum_s[pl.ds(pl.multiple_of(blk * N_HEADS, N_HEADS), N_HEADS), :] = ksum
        mp = s_tiles[0].max(axis=-1, keepdims=True)
        for st in s_tiles[1:]:
            mp = jnp.maximum(mp, jnp.max(st, axis=-1, keepdims=True))
        lsum = jnp.zeros((HT_ROWS, 1), F32)
        acc = jnp.zeros((HT_ROWS, HEAD_DIM), F32)
        for st, vt in zip(s_tiles, v_tiles):
            e = jnp.exp(st - mp)
            lsum = lsum + jnp.sum(e, axis=-1, keepdims=True)
            acc = acc + jnp.dot(e.astype(BF16), vt, preferred_element_type=F32)
        m_all[blk] = mp
        l_all[blk] = lsum
        acc_all[blk] = acc

    @pl.when(p == pl.num_programs(1) - 1)
    def _():
        kmean = ksum_s[...] * (1.0 / MOBA_BLOCK)
        g = _nt(qf_s[...], kmean, HI)
        same = (_iota((HT_ROWS, LANES), 1) & (N_HEADS - 1)) == (_iota((HT_ROWS, LANES), 0) // SUBLANES)
        group = ((_iota((LANES, LANES), 0) // N_HEADS) == _iota((LANES, LANES), 1)).astype(F32)
        gate = jnp.dot(jnp.where(same, g, 0.0), group, precision=HI, preferred_element_type=F32)
        sel = _top_blocks(gate, n_blocks)
        so = _nt(qb_s[...], _pad_rows(_stack_heads(kn_ref[...])).astype(BF16))
        so = jnp.where(_visible_new(), so, NEG_INF)
        m_own = jnp.max(so, axis=-1, keepdims=True)
        e_own = jnp.exp(so - m_own)
        l_own = jnp.sum(e_own, axis=-1, keepdims=True)
        acc_own = jnp.dot(e_own.astype(BF16), _pad_rows(_stack_heads(vn_ref[...])).astype(BF16),
                          preferred_element_type=F32)
        picks = [sel[:, n:n + 1] > 0.0 for n in range(n_blocks)]
        m_tot = m_own
        for n, pick in enumerate(picks):
            m_tot = jnp.maximum(m_tot, jnp.where(pick, m_all[n], NEG_INF))
        w_own = jnp.exp(m_own - m_tot)
        acc = w_own * acc_own
        l_tot = w_own * l_own
        for n, pick in enumerate(picks):
            w = jnp.where(pick, jnp.exp(m_all[n] - m_tot), 0.0)
            acc = acc + w * acc_all[n]
            l_tot = l_tot + w * l_all[n]
        o_ref[...] = _unstack_heads(acc / l_tot)


def _moba_sample(page_table, q, kn, vn, kc, vc, slot):
    n = q.shape[0]
    nb, npages = page_table.shape
    pages_per_block = MOBA_BLOCK // (kc.shape[2] // N_HEADS)
    n_blocks = npages // pages_per_block
    group = PAGE_GROUP
    assert n_blocks * N_HEADS <= LANES and group % pages_per_block == 0 and npages % group == 0

    def rows(w):
        return pl.BlockSpec((SUBLANES, w), lambda b, p, pt: (b, 0))

    pages = _page_specs(slot, group)
    grid_spec = pltpu.PrefetchScalarGridSpec(
        num_scalar_prefetch=1,
        grid=(nb, npages // group),
        in_specs=[rows(D_MODEL), rows(D_MODEL), rows(D_MODEL)] + pages + pages,
        out_specs=rows(D_MODEL),
        scratch_shapes=[pltpu.VMEM((HT_ROWS, HEAD_DIM), F32), pltpu.VMEM((HT_ROWS, HEAD_DIM), BF16),
                        pltpu.VMEM((n_blocks, HT_ROWS, 1), F32), pltpu.VMEM((n_blocks, HT_ROWS, 1), F32),
                        pltpu.VMEM((n_blocks, HT_ROWS, HEAD_DIM), F32),
                        pltpu.VMEM((LANES, HEAD_DIM), F32)])
    return pl.pallas_call(
        functools.partial(_moba_sample_kernel, group=group, pages_per_block=pages_per_block),
        grid_spec=grid_spec,
        out_shape=jax.ShapeDtypeStruct((n, D_MODEL), F32),
        compiler_params=_params("parallel", "arbitrary"),
        name="moba_sample",
    )(page_table, q, kn, vn, *([kc] * group), *([vc] * group))


def _pad_lanes(a, width=LANES):
    return jnp.pad(a, [(0, 0)] * (a.ndim - 1) + [(0, width - a.shape[-1])])


def _rope_tables(pos):
    half = HEAD_DIM // 2
    inv = ROPE_THETA ** (-jnp.arange(half, dtype=F32) / half)
    ang = pos.astype(F32)[:, None] * inv[None, :]
    cos, sin = jnp.cos(ang), jnp.sin(ang)
    return jnp.concatenate([cos, cos], axis=-1), jnp.concatenate([-sin, sin], axis=-1)


def kernel(x_prompt, x_sample, cache_fox_k, cache_fox_v, cache_fox_logf, cache_moba_k, cache_moba_v,
           state_mlstm_c, state_mlstm_n, state_mlstm_m, state_ffn_conv, page_table,
           norm_mix_g, norm_ffn_g, norm_final_g,
           mlstm_w_in, mlstm_b_gates, mlstm_norm_g, mlstm_w_out,
           fox_w_in, fox_b_f, fox_w_out, moba_w_in, moba_w_out,
           ffn_w_up, ffn_conv_w, ffn_conv_b, ffn_w_down):
    B, S, D = x_prompt.shape
    Bd, T, _ = x_sample.shape
    H, Dh, dk = N_HEADS, HEAD_DIM, MLSTM_QK
    depth = norm_mix_g.shape[0]
    n_pool, page_rows = cache_fox_k.shape[1], cache_fox_k.shape[2]
    past = page_table.shape[1] * page_rows
    assert T == SUBLANES and D == D_MODEL and page_rows == LANES

    xp = x_prompt.reshape(B * S, D)
    xs = x_sample.reshape(Bd * T, D)
    fkp, fks, fvp, fvs, flp, fls = [], [], [], [], [], []
    mkp, mks, mvp, mvs = [], [], [], []
    acp, acs, anp, ans, amp, ams = [], [], [], [], [], []
    cvp, cvs = [], []

    for i in range(depth):
        kind, slot = i % N_MIXERS, i // N_MIXERS
        g_mix = norm_mix_g[i]
        if kind == 0:
            w = mlstm_w_in[slot]
            wq = w[:, :H * dk] * (dk ** -0.5)
            wk = w[:, H * dk:2 * H * dk]
            rest = w[:, 2 * H * dk:2 * H * dk + 2 * H * Dh]
            wig = _pad_lanes(w[:, 2 * H * dk + 2 * H * Dh:2 * H * dk + 2 * H * Dh + H])
            wfg = _pad_lanes(w[:, 2 * H * dk + 2 * H * Dh + H:])
            pad_heads = lambda a: _pad_lanes(a.reshape(D, H, dk), Dh).reshape(D, H * Dh)
            w_p = jnp.concatenate([pad_heads(wq), pad_heads(wk), rest, wig, wfg], axis=1).astype(BF16)
            w_s = jnp.concatenate([wq, wk, rest, wig, wfg], axis=1).astype(BF16)
            bi = _pad_lanes(mlstm_b_gates[slot][:H].reshape(1, H))
            bfg = _pad_lanes(mlstm_b_gates[slot][H:].reshape(1, H))
            ng = mlstm_norm_g[slot].reshape(1, H * Dh)
            wo = mlstm_w_out[slot].astype(BF16)

            q, k, v, o, ig, fg = _norm_proj(xp, g_mix, w_p, [H * Dh, H * Dh, H * Dh, H * Dh, LANES, LANES])
            hp, c_, n_, m_ = _mlstm_prompt(q, k, v, o, ig, fg, bi, bfg, ng, B)
            acp.append(c_); anp.append(n_[:, :, :dk]); amp.append(m_.reshape(B, H))
            xp = _proj_res(hp, wo, xp)

            q, k, v, o, ig, fg = _norm_proj(xs, g_mix, w_s, [H * dk, H * dk, H * Dh, H * Dh, LANES, LANES])
            hs_, c_, n_, m_ = _mlstm_sample(q, k, v, o, ig, fg, bi, bfg, ng,
                                            state_mlstm_c[slot], state_mlstm_n[slot], state_mlstm_m[slot])
            acs.append(c_); ans.append(n_.reshape(Bd, H, dk)); ams.append(m_.reshape(Bd, H))
            xs = _proj_res(hs_, wo, xs)
        elif kind == 1:
            w = fox_w_in[slot]
            w_b = jnp.concatenate([w[:, :3 * H * Dh], _pad_lanes(w[:, 3 * H * Dh:])], axis=1).astype(BF16)
            bfg = _pad_lanes(fox_b_f[slot].reshape(1, H))
            wo = fox_w_out[slot].astype(BF16)
            widths = [H * Dh, H * Dh, H * Dh, LANES]

            q, k, v, fraw = _norm_proj(xp, g_mix, w_b, widths)
            lf, crow = _fox_gates(fraw, bfg, B)
            op = _fox_prompt(q, k, v, crow, B)
            fkp.append(k.reshape(B, S, H, Dh)); fvp.append(v.reshape(B, S, H, Dh))
            flp.append(lf[:, :H].reshape(B, S, H))
            xp = _proj_res(op, wo, xp)

            q, k, v, fraw = _norm_proj(xs, g_mix, w_b, widths)
            fflat = _pad_lanes(fraw[:, :H].reshape(Bd, T, H).transpose(0, 2, 1).reshape(Bd, 1, H * T))
            bflat = _pad_lanes(jnp.repeat(fox_b_f[slot], T).reshape(1, H * T))
            n_slots = cache_fox_k.shape[0]
            os_, lf = _fox_sample(page_table, q, k, v, fflat, bflat,
                                  cache_fox_logf[slot].astype(F32).reshape(n_pool, page_rows * H),
                                  cache_fox_k.astype(F32).reshape(n_slots, n_pool, page_rows * H, Dh),
                                  cache_fox_v.astype(F32).reshape(n_slots, n_pool, page_rows * H, Dh), slot)
            fks.append(k.reshape(Bd, T, H, Dh)); fvs.append(v.reshape(Bd, T, H, Dh))
            fls.append(lf[:, 0, :H * T].reshape(Bd, H, T).transpose(0, 2, 1))
            xs = _proj_res(os_, wo, xs)
        else:
            w_b = moba_w_in[slot].astype(BF16)
            wo = moba_w_out[slot].astype(BF16)
            widths = [H * Dh, H * Dh, H * Dh]
            cos_p, sin_p = _rope_tables(jnp.arange(S, dtype=jnp.int32))
            cos_s, sin_s = _rope_tables(past + jnp.arange(T, dtype=jnp.int32))
            reps = min(MOBA_BLOCK, Bd * T) // T
            cos_s, sin_s = jnp.tile(cos_s, (reps, 1)), jnp.tile(sin_s, (reps, 1))

            q, k, v = _norm_proj(xp, g_mix, w_b, widths)
            q, k, km = _rope(q, k, cos_p, sin_p)
            nblk = S // MOBA_BLOCK
            km = jnp.pad(km.reshape(B, nblk, D), ((0, 0), (0, -nblk % SUBLANES), (0, 0)))
            op = _moba_prompt(q, k, v, km, B)
            mkp.append(k.reshape(B, S, H, Dh)); mvp.append(v.reshape(B, S, H, Dh))
            xp = _proj_res(op, wo, xp)

            q, k, v = _norm_proj(xs, g_mix, w_b, widths)
            q, k, _ = _rope(q, k, cos_s, sin_s)
            n_slots = cache_moba_k.shape[0]
            os_ = _moba_sample(page_table, q, k, v,
                               cache_moba_k.astype(F32).reshape(n_slots, n_pool, page_rows * H, Dh),
                               cache_moba_v.astype(F32).reshape(n_slots, n_pool, page_rows * H, Dh), slot)
            mks.append(k.reshape(Bd, T, H, Dh)); mvs.append(v.reshape(Bd, T, H, Dh))
            xs = _proj_res(os_, wo, xs)

        wup = ffn_w_up[i].astype(BF16)
        wdn = ffn_w_down[i].astype(BF16)
        xp, tail = _ffn_prompt(xp, norm_ffn_g[i], wup, ffn_conv_w[i], ffn_conv_b[i], wdn, B)
        cvp.append(tail.reshape(B, SUBLANES, 2 * D_FF)[:, SUBLANES - (CONV_W - 1):])
        p2 = jnp.pad(state_ffn_conv[i], ((0, 0), (0, T - (CONV_W - 1)), (0, 0))).reshape(Bd * T, 2 * D_FF)
        xs, u = _ffn_sample(xs, norm_ffn_g[i], wup, ffn_conv_w[i], ffn_conv_b[i], wdn, p2)
        cvs.append(u.reshape(Bd, T, 2 * D_FF)[:, T - (CONV_W - 1):])

    y_prompt = _final_norm(xp, norm_final_g).reshape(B, S, D)
    y_sample = _final_norm(xs, norm_final_g).reshape(Bd, T, D)
    return (y_prompt, y_sample,
            jnp.stack(fkp), jnp.stack(fks), jnp.stack(fvp), jnp.stack(fvs), jnp.stack(flp), jnp.stack(fls),
            jnp.stack(mkp), jnp.stack(mks), jnp.stack(mvp), jnp.stack(mvs),
            jnp.stack(acp), jnp.stack(acs), jnp.stack(anp), jnp.stack(ans), jnp.stack(amp), jnp.stack(ams),
            jnp.stack(cvp), jnp.stack(cvs))
```

```python
import functools

import jax
import jax.numpy as jnp
from jax import lax
from jax.experimental import pallas as pl
from jax.experimental.pallas import tpu as pltpu

F32 = jnp.float32
BF16 = jnp.bfloat16
HI = lax.Precision.HIGHEST

D_MODEL = 1024
N_HEADS = 8
HEAD_DIM = 128
MLSTM_QK = 64
D_FF = 2816
CONV_W = 3
N_MIXERS = 3
MOBA_BLOCK = 256
MOBA_TOPK = 3
ROPE_THETA = 10000.0
RMS_EPS = 1e-6
NEG_INF = float("-inf")
M_FLOOR = -1e30
LANES = 128
SUBLANES = 8
VMEM_LIMIT = 56 * 1024 * 1024

NT_DIMS = (((1,), (1,)), ((), ()))
TN_DIMS = (((0,), (0,)), ((), ()))


def _params(*sem):
    return pltpu.CompilerParams(dimension_semantics=sem, vmem_limit_bytes=VMEM_LIMIT)


def _resident(shape):
    nd = len(shape)
    return pl.BlockSpec(shape, lambda *_: (0,) * nd, pipeline_mode=pl.Buffered(1))


def _rms(x, g):
    return x * lax.rsqrt(jnp.mean(x * x, axis=-1, keepdims=True) + RMS_EPS) * g


def _iota(shape, axis):
    return lax.broadcasted_iota(jnp.int32, shape, axis)


def _eye_rows(n):
    return (_iota((n, LANES), 0) == _iota((n, LANES), 1)).astype(F32)


def _nt(a, b, precision=None):
    return lax.dot_general(a, b, NT_DIMS, precision=precision, preferred_element_type=F32)


def _norm_proj_kernel(x_ref, g_ref, w_ref, *o_refs, widths):
    xn = _rms(x_ref[...], g_ref[0]).astype(BF16)
    off = 0
    for o_ref, wd in zip(o_refs, widths):
        for c in range(0, wd, 512):
            cw = min(512, wd - c)
            o_ref[:, c:c + cw] = jnp.dot(xn, w_ref[:, off + c:off + c + cw],
                                         preferred_element_type=F32)
        off += wd


def _norm_proj(x, g, layer, w, widths, tm=256):
    n = x.shape[0]
    tm = min(tm, n)
    assert sum(widths) == w.shape[1] and n % tm == 0
    return pl.pallas_call(
        functools.partial(_norm_proj_kernel, widths=tuple(widths)),
        grid=(n // tm,),
        in_specs=[pl.BlockSpec((tm, D_MODEL), lambda i: (i, 0)),
                  _layer_block(g, layer),
                  _resident(w.shape)],
        out_specs=[pl.BlockSpec((tm, wd), lambda i: (i, 0)) for wd in widths],
        out_shape=[jax.ShapeDtypeStruct((n, wd), F32) for wd in widths],
        compiler_params=_params("parallel"),
        name="norm_proj",
    )(x, g, w)


def _proj_res_kernel(h_ref, w_ref, x_ref, o_ref):
    o_ref[...] = x_ref[...] + jnp.dot(h_ref[...].astype(BF16), w_ref[...],
                                      preferred_element_type=F32)


def _proj_res(h, w, x, tm=512):
    n = x.shape[0]
    tm = min(tm, n)
    return pl.pallas_call(
        _proj_res_kernel,
        grid=(n // tm,),
        in_specs=[pl.BlockSpec((tm, h.shape[1]), lambda i: (i, 0)),
                  _resident(w.shape),
                  pl.BlockSpec((tm, D_MODEL), lambda i: (i, 0))],
        out_specs=pl.BlockSpec((tm, D_MODEL), lambda i: (i, 0)),
        out_shape=jax.ShapeDtypeStruct((n, D_MODEL), F32),
        compiler_params=_params("parallel"),
        name="proj_res",
    )(h, w, x)


def _final_norm_kernel(x_ref, g_ref, o_ref):
    o_ref[...] = _rms(x_ref[...], g_ref[...])


def _final_norm(x, g, tm=512):
    n = x.shape[0]
    tm = min(tm, n)
    return pl.pallas_call(
        _final_norm_kernel,
        grid=(n // tm,),
        in_specs=[pl.BlockSpec((tm, D_MODEL), lambda i: (i, 0)), _resident((1, D_MODEL))],
        out_specs=pl.BlockSpec((tm, D_MODEL), lambda i: (i, 0)),
        out_shape=jax.ShapeDtypeStruct((n, D_MODEL), F32),
        compiler_params=_params("parallel"),
        name="final_norm",
    )(x, g.reshape(1, D_MODEL))


FFN_CH = 256


def _layer_block(arr, layer):
    nd = arr.ndim
    return pl.BlockSpec((1,) + arr.shape[1:], lambda *_: (layer,) + (0,) * (nd - 1),
                        pipeline_mode=pl.Buffered(1))


def _ffn_kernel(*refs, tt, grouped):
    width = 2 * D_FF
    if grouped:
        (x_ref, g_ref, wup_ref, cw_ref, cb_ref, wdn_ref, prev_ref, o_ref, new_ref,
         hbuf, p_s, u_s) = refs
        groups = tt // SUBLANES
        p_s[...] = jnp.zeros_like(p_s)
    else:
        x_ref, g_ref, wup_ref, cw_ref, cb_ref, wdn_ref, o_ref, tail_ref, hbuf, carry = refs

        @pl.when(pl.program_id(1) == 0)
        def _():
            carry[...] = jnp.zeros_like(carry)

    x = x_ref[...]
    xn = _rms(x, g_ref[0]).astype(BF16)
    row = _iota((tt, 1), 0)
    for c in range(D_FF // FFN_CH):
        ys = []
        for part in range(2):
            c0 = part * D_FF + c * FFN_CH
            cols = slice(c0, c0 + FFN_CH)
            u = jnp.dot(xn, wup_ref[0, :, cols], preferred_element_type=F32)
            r1 = pltpu.roll(u, 1, 0)
            r2 = pltpu.roll(u, 2, 0)
            if grouped:
                for k in range(FFN_CH // LANES):
                    lo = c0 + k * LANES
                    p_s[k, pl.ds(0, groups, stride=SUBLANES), :] = prev_ref[0, :, lo:lo + LANES]
                    p_s[k, pl.ds(1, groups, stride=SUBLANES), :] = prev_ref[0, :, width + lo:width + lo + LANES]
                    u_s[k] = u[:, k * LANES:(k + 1) * LANES]
                    new_ref[:, lo:lo + LANES] = u_s[k, pl.ds(SUBLANES - 2, groups, stride=SUBLANES), :]
                    new_ref[:, width + lo:width + lo + LANES] = (
                        u_s[k, pl.ds(SUBLANES - 1, groups, stride=SUBLANES), :])
                p2 = jnp.concatenate([p_s[k] for k in range(FFN_CH // LANES)], axis=1)
                p1 = pltpu.roll(p2, tt - 1, 0)
                t8 = row & (SUBLANES - 1)
                um1 = jnp.where(t8 < 1, p1, r1)
                um2 = jnp.where(t8 < 2, p2, r2)
            else:
                pc = carry[:, cols]
                row8 = row[:SUBLANES]
                f1 = jnp.where(row8 < 1, pltpu.roll(pc, 1, 0), r1[:SUBLANES])
                f2 = jnp.where(row8 < 2, pltpu.roll(pc, 2, 0), r2[:SUBLANES])
                um1 = jnp.concatenate([f1, r1[SUBLANES:]], axis=0)
                um2 = jnp.concatenate([f2, r2[SUBLANES:]], axis=0)
                tail = u[tt - SUBLANES:]
                carry[:, cols] = tail
                tail_ref[:, cols] = tail
            y = cb_ref[0, :, cols] + cw_ref[0, 0:1, cols] * um2
            y = y + cw_ref[0, 1:2, cols] * um1
            y = y + cw_ref[0, 2:3, cols] * u
            ys.append(y)
        gate, val = ys
        hbuf[:, c * FFN_CH:(c + 1) * FFN_CH] = (gate * jax.nn.sigmoid(gate) * val).astype(BF16)
    o_ref[...] = x + jnp.dot(hbuf[...], wdn_ref[0], preferred_element_type=F32)


def _ffn_params(g, wup, cw, cb, wdn, layer):
    return [_layer_block(a, layer) for a in (g, wup, cw, cb, wdn)]


def _ffn_prompt(x, params, layer, nseq, tt=512):
    n = x.shape[0]
    nt = n // nseq // tt
    return pl.pallas_call(
        functools.partial(_ffn_kernel, tt=tt, grouped=False),
        grid=(nseq, nt),
        in_specs=[pl.BlockSpec((tt, D_MODEL), lambda b, t: (b * nt + t, 0))] + _ffn_params(*params, layer),
        out_specs=[pl.BlockSpec((tt, D_MODEL), lambda b, t: (b * nt + t, 0)),
                   pl.BlockSpec((SUBLANES, 2 * D_FF), lambda b, t: (b, 0))],
        out_shape=[jax.ShapeDtypeStruct((n, D_MODEL), F32),
                   jax.ShapeDtypeStruct((nseq * SUBLANES, 2 * D_FF), F32)],
        scratch_shapes=[pltpu.VMEM((tt, D_FF), BF16), pltpu.VMEM((SUBLANES, 2 * D_FF), F32)],
        compiler_params=_params("parallel", "arbitrary"),
        name="ffn_prompt",
    )(x, *params)


def _ffn_sample(x, params, layer, prev, tt=256):
    n = x.shape[0]
    tt = min(tt, n)
    groups = tt // SUBLANES
    return pl.pallas_call(
        functools.partial(_ffn_kernel, tt=tt, grouped=True),
        grid=(n // tt,),
        in_specs=[pl.BlockSpec((tt, D_MODEL), lambda i: (i, 0))] + _ffn_params(*params, layer)
                 + [pl.BlockSpec((1, groups, 4 * D_FF), lambda i: (layer, i, 0))],
        out_specs=[pl.BlockSpec((tt, D_MODEL), lambda i: (i, 0)),
                   pl.BlockSpec((groups, 4 * D_FF), lambda i: (i, 0))],
        out_shape=[jax.ShapeDtypeStruct((n, D_MODEL), F32),
                   jax.ShapeDtypeStruct((n // SUBLANES, 4 * D_FF), F32)],
        scratch_shapes=[pltpu.VMEM((tt, D_FF), BF16), pltpu.VMEM((FFN_CH // LANES, tt, LANES), F32),
                        pltpu.VMEM((FFN_CH // LANES, tt, LANES), F32)],
        compiler_params=_params("parallel"),
        name="ffn_sample",
    )(x, *params, prev)


def _mlstm_prompt_kernel(q_ref, k_ref, v_ref, o_ref, ig_ref, fg_ref, bi_ref, bf_ref, ng_ref,
                         h_ref, cout_ref, nout_ref, mout_ref, c_s, m_s, *, chunk):
    L = chunk
    j = pl.program_id(1)

    def tile(x, n):
        return jnp.concatenate([x] * n, axis=1)

    @pl.when(j == 0)
    def _():
        c_s[...] = jnp.zeros_like(c_s)
        m_s[...] = jnp.zeros_like(m_s)

    ig = ig_ref[...] + bi_ref[...]
    lf = jax.nn.log_sigmoid(fg_ref[...] + bf_ref[...])
    tril = (_iota((L, L), 0) >= _iota((L, L), 1))
    bt = jnp.dot(tril.astype(F32), lf, precision=HI, preferred_element_type=F32)
    rc = ig - bt
    r_t = _nt(_eye_rows(SUBLANES), rc, HI)
    mean_mat = jnp.full((HEAD_DIM, HEAD_DIM), 1.0 / HEAD_DIM, F32)
    ones_b = jnp.ones((L, LANES), BF16)
    for h in range(N_HEADS):
        hs = slice(h * HEAD_DIM, (h + 1) * HEAD_DIM)
        kh = k_ref[:, hs]
        qb, kb = q_ref[:, hs].astype(BF16), kh.astype(BF16)
        v_ones = jnp.concatenate([v_ref[:, hs].astype(BF16), ones_b], axis=1)
        bcol = jnp.broadcast_to(bt[:, h:h + 1], (L, LANES))
        rcol = jnp.broadcast_to(rc[:, h:h + 1], (L, LANES))
        dm = jnp.where(tril, tile(bcol, L // LANES) + r_t[h:h + 1, :], NEG_INF)
        m_prev = m_s[h]
        inter = bcol + m_prev
        mt = jnp.maximum(jnp.max(dm, axis=-1, keepdims=True), inter)
        w_intra = jnp.exp(dm - tile(mt, L // LANES))
        w_inter = jnp.exp(inter - mt)
        c_prev = c_s[h]
        s = _nt(qb, kb) * w_intra
        tot = (jnp.dot(s.astype(BF16), v_ones, preferred_element_type=F32)
               + jnp.dot(qb, c_prev.astype(BF16), preferred_element_type=F32) * tile(w_inter, 2))
        den = jnp.maximum(jnp.abs(tot[:, HEAD_DIM:]), jnp.exp(-mt))
        hh = tot[:, :HEAD_DIM] / den
        ms = jnp.dot(hh * hh, mean_mat, precision=HI, preferred_element_type=F32)
        hh = hh * lax.rsqrt(ms + RMS_EPS) * ng_ref[:, hs]
        h_ref[:, hs] = hh * jax.nn.sigmoid(o_ref[:, hs])
        b_last = bcol[L - 1:L, :]
        gcol = b_last + rcol
        m_new = jnp.maximum(b_last + m_prev, jnp.max(gcol, axis=0, keepdims=True))
        a_prev = jnp.exp(b_last + m_prev - m_new)
        ka = kh * jnp.exp(gcol - m_new)
        c_s[h] = tile(a_prev, 2) * c_prev + lax.dot_general(ka.astype(BF16), v_ones, TN_DIMS,
                                                            preferred_element_type=F32)
        m_s[h] = m_new

    @pl.when(j == pl.num_programs(1) - 1)
    def _():
        for h in range(N_HEADS):
            cout_ref[0, h] = c_s[h][:MLSTM_QK, :HEAD_DIM]
            nout_ref[0, h:h + 1, :] = c_s[h][:, HEAD_DIM:].T[0:1, :]
            mout_ref[0, :, h:h + 1] = m_s[h][:, 0:1]


def _mlstm_prompt(q, k, v, o, ig, fg, bi, bfg, ng, nseq, chunk=256):
    n = q.shape[0]
    nc = n // nseq // chunk
    wide = pl.BlockSpec((chunk, D_MODEL), lambda b, j: (b * nc + j, 0))
    narrow = pl.BlockSpec((chunk, LANES), lambda b, j: (b * nc + j, 0))
    return pl.pallas_call(
        functools.partial(_mlstm_prompt_kernel, chunk=chunk),
        grid=(nseq, nc),
        in_specs=[wide, wide, wide, wide, narrow, narrow,
                  _resident((1, LANES)), _resident((1, LANES)), _resident((1, D_MODEL))],
        out_specs=[wide,
                   pl.BlockSpec((1, N_HEADS, MLSTM_QK, HEAD_DIM), lambda b, j: (b, 0, 0, 0)),
                   pl.BlockSpec((1, N_HEADS, LANES), lambda b, j: (b, 0, 0)),
                   pl.BlockSpec((1, 1, N_HEADS), lambda b, j: (b, 0, 0))],
        out_shape=[jax.ShapeDtypeStruct((n, D_MODEL), F32),
                   jax.ShapeDtypeStruct((nseq, N_HEADS, MLSTM_QK, HEAD_DIM), F32),
                   jax.ShapeDtypeStruct((nseq, N_HEADS, LANES), F32),
                   jax.ShapeDtypeStruct((nseq, 1, N_HEADS), F32)],
        scratch_shapes=[pltpu.VMEM((N_HEADS, HEAD_DIM, 2 * HEAD_DIM), F32),
                        pltpu.VMEM((N_HEADS, 1, LANES), F32)],
        compiler_params=_params("parallel", "arbitrary"),
        name="mlstm_prompt",
    )(q, k, v, o, ig, fg, bi, bfg, ng)


def _cumsum_rows8(x):
    row = _iota(x.shape, 0)
    for sh in (1, 2, 4):
        x = x + jnp.where(row >= sh, pltpu.roll(x, sh, 0), 0.0)
    return x


def _stack_cols(x, n=N_HEADS):
    return jnp.concatenate([x[:, h:h + 1] for h in range(n)], axis=0)


def _stack_bcast(x, rows, n=N_HEADS):
    return jnp.concatenate([jnp.broadcast_to(x[:, h:h + 1], (rows, 1)) for h in range(n)], axis=0)


def _pad_rows(x, rows=LANES):
    return jnp.concatenate([x, jnp.zeros((rows - x.shape[0], x.shape[1]), x.dtype)], axis=0)


def _mlstm_sample_kernel(q_ref, k_ref, v_ref, o_ref, ig_ref, fg_ref, bi_ref, bf_ref, ng_ref,
                         c0_ref, n0_ref, m0_ref, h_ref, cout_ref, nout_ref, mout_ref):
    T, HT, QW = SUBLANES, N_HEADS * SUBLANES, N_HEADS * MLSTM_QK
    ig = ig_ref[...] + bi_ref[...]
    lf = jax.nn.log_sigmoid(fg_ref[...] + bf_ref[...])
    bt = _cumsum_rows8(lf)
    rc = ig - bt
    m0 = m0_ref[0]
    bcol = _stack_cols(bt)
    r_t = _nt(_eye_rows(SUBLANES), _pad_rows(rc), HI)
    rrow = jnp.concatenate([jnp.broadcast_to(r_t[h:h + 1, :], (T, LANES))
                            for h in range(N_HEADS)], axis=0)
    lane = _iota((HT, LANES), 1)
    tok = _iota((HT, LANES), 0) & (T - 1)
    dm = jnp.where(lane <= tok, bcol + rrow, NEG_INF)
    inter = bcol + _stack_bcast(m0, T)
    mt = jnp.maximum(jnp.max(dm, axis=-1, keepdims=True), inter)
    w_intra = jnp.exp(dm - mt)
    w_inter = jnp.exp(inter - mt)

    head_of_row = _iota((HT, QW), 0) // T
    head_of_lane = _iota((HT, QW), 1) // MLSTM_QK
    diag = head_of_row == head_of_lane
    q8, k8, v8 = q_ref[...], k_ref[...], v_ref[...]
    qbd = jnp.where(diag, jnp.concatenate([q8] * N_HEADS, axis=0), 0.0)
    qbd_b = qbd.astype(BF16)
    s = _nt(qbd_b, _pad_rows(k8).astype(BF16)) * w_intra
    nv = jnp.dot(s.astype(BF16), _pad_rows(v8).astype(BF16), preferred_element_type=F32)
    row_head = _iota((HT, HEAD_DIM), 0) // T
    num = jnp.zeros((HT, HEAD_DIM), F32)
    for h in range(N_HEADS):
        num = num + jnp.where(row_head == h, nv[:, h * HEAD_DIM:(h + 1) * HEAD_DIM], 0.0)
    c_prev = c0_ref[0].reshape(QW, HEAD_DIM)
    n_prev = n0_ref[0]
    num = num + jnp.dot(qbd_b, c_prev.astype(BF16), preferred_element_type=F32) * w_inter
    den = (jnp.sum(s, axis=-1, keepdims=True)
           + jnp.sum(qbd * n_prev, axis=-1, keepdims=True) * w_inter)
    den = jnp.maximum(jnp.abs(den), jnp.exp(-mt))
    hh = num / den
    ng = jnp.concatenate([jnp.broadcast_to(ng_ref[:, h * HEAD_DIM:(h + 1) * HEAD_DIM], (T, HEAD_DIM))
                          for h in range(N_HEADS)], axis=0)
    o8 = o_ref[...]
    ost = jnp.concatenate([o8[:, h * HEAD_DIM:(h + 1) * HEAD_DIM] for h in range(N_HEADS)], axis=0)
    hh = hh * lax.rsqrt(jnp.mean(hh * hh, axis=-1, keepdims=True) + RMS_EPS) * ng
    hh = hh * jax.nn.sigmoid(ost)
    h_ref[...] = jnp.concatenate([hh[h * T:(h + 1) * T, :] for h in range(N_HEADS)], axis=1)

    b_last = bt[T - 1:T, :]
    g2 = b_last + rc
    m_new = jnp.maximum(b_last + m0, jnp.max(g2, axis=0, keepdims=True))
    a_prev = jnp.exp(b_last + m0 - m_new)
    a_tok = jnp.exp(g2 - m_new)
    kabd = jnp.where(diag, jnp.concatenate([k8] * N_HEADS, axis=0) * _stack_cols(a_tok), 0.0)
    vst = jnp.concatenate([v8[:, h * HEAD_DIM:(h + 1) * HEAD_DIM] for h in range(N_HEADS)], axis=0)
    dc = lax.dot_general(kabd.astype(BF16), vst.astype(BF16), TN_DIMS, preferred_element_type=F32)
    c_new = _stack_bcast(a_prev, MLSTM_QK) * c_prev + dc
    cout_ref[0] = c_new.reshape(N_HEADS, MLSTM_QK, HEAD_DIM)
    a_lane = jnp.max(jnp.where(diag, _stack_bcast(a_prev, T), 0.0), axis=0, keepdims=True)
    nout_ref[0] = a_lane * n_prev + jnp.sum(kabd, axis=0, keepdims=True)
    mout_ref[0] = m_new[:, :N_HEADS]


def _mlstm_sample(q, k, v, o, ig, fg, bi, bfg, ng, c0, n0, m0):
    n = q.shape[0]
    nb = n // SUBLANES
    qw = N_HEADS * MLSTM_QK

    def rows(w):
        return pl.BlockSpec((SUBLANES, w), lambda b: (b, 0))

    cspec = pl.BlockSpec((1, N_HEADS, MLSTM_QK, HEAD_DIM), lambda b: (b, 0, 0, 0))
    nspec = pl.BlockSpec((1, 1, qw), lambda b: (b, 0, 0))
    mspec = pl.BlockSpec((1, 1, N_HEADS), lambda b: (b, 0, 0))
    return pl.pallas_call(
        _mlstm_sample_kernel,
        grid=(nb,),
        in_specs=[rows(qw), rows(qw), rows(D_MODEL), rows(D_MODEL), rows(LANES), rows(LANES),
                  _resident((1, LANES)), _resident((1, LANES)), _resident((1, D_MODEL)),
                  cspec, nspec, pl.BlockSpec((1, 1, LANES), lambda b: (b, 0, 0))],
        out_specs=[rows(D_MODEL), cspec, nspec, mspec],
        out_shape=[jax.ShapeDtypeStruct((n, D_MODEL), F32),
                   jax.ShapeDtypeStruct((nb, N_HEADS, MLSTM_QK, HEAD_DIM), F32),
                   jax.ShapeDtypeStruct((nb, 1, qw), F32),
                   jax.ShapeDtypeStruct((nb, 1, N_HEADS), F32)],
        compiler_params=_params("parallel"),
        name="mlstm_sample",
    )(q, k, v, o, ig, fg, bi, bfg, ng, c0, n0.reshape(nb, 1, qw), _pad_lanes(m0).reshape(nb, 1, LANES))


def _fox_gate_kernel(f_ref, bf_ref, lf_ref, crow_ref, carry, *, ts):
    @pl.when(pl.program_id(1) == 0)
    def _():
        carry[...] = jnp.zeros_like(carry)

    lf = jax.nn.log_sigmoid(f_ref[...] + bf_ref[...])
    tril = (_iota((ts, ts), 0) >= _iota((ts, ts), 1)).astype(F32)
    c = jnp.dot(tril, lf, precision=HI, preferred_element_type=F32) + carry[...]
    carry[...] = c[ts - 1:ts, :]
    lf_ref[...] = lf
    crow_ref[0] = _nt(_eye_rows(SUBLANES), c, HI)


def _fox_gates(fraw, bfg, nseq, ts=512):
    n = fraw.shape[0]
    seq = n // nseq
    nt = seq // ts
    tok = pl.BlockSpec((ts, LANES), lambda b, t: (b * nt + t, 0))
    return pl.pallas_call(
        functools.partial(_fox_gate_kernel, ts=ts),
        grid=(nseq, nt),
        in_specs=[tok, _resident((1, LANES))],
        out_specs=[tok, pl.BlockSpec((1, SUBLANES, ts), lambda b, t: (b, 0, t))],
        out_shape=[jax.ShapeDtypeStruct((n, LANES), F32),
                   jax.ShapeDtypeStruct((nseq, SUBLANES, seq), F32)],
        scratch_shapes=[pltpu.VMEM((1, LANES), F32)],
        compiler_params=_params("parallel", "arbitrary"),
        name="fox_gates",
    )(fraw, bfg)


def _online_update(s, vb, m_ref, l_ref, acc_ref, hs):
    s_tiles, v_tiles = (s, vb) if isinstance(s, (list, tuple)) else ([s], [vb])
    m_prev = m_ref[...]
    m_new = m_prev
    for st in s_tiles:
        m_new = jnp.maximum(m_new, jnp.max(st, axis=-1, keepdims=True))
    alpha = jnp.exp(m_prev - m_new)
    l_new = alpha * l_ref[...]
    acc = alpha * acc_ref[:, hs]
    for st, vt in zip(s_tiles, v_tiles):
        p = jnp.exp(st - m_new)
        l_new = l_new + jnp.sum(p, axis=-1, keepdims=True)
        acc = acc + jnp.dot(p.astype(BF16), vt, preferred_element_type=F32)
    l_ref[...] = l_new
    acc_ref[:, hs] = acc
    m_ref[...] = m_new


def _flash_update(s, v, m_ref, l_ref, acc_ref, hs):
    m_prev = m_ref[...]
    m_new = jnp.maximum(m_prev, jnp.max(s, axis=-1, keepdims=True))
    alpha = jnp.exp(m_prev - m_new)
    p = jnp.exp(s - jnp.concatenate([m_new] * (s.shape[1] // LANES), axis=1))
    v_ones = jnp.concatenate([v.astype(BF16), jnp.ones((v.shape[0], LANES), BF16)], axis=1)
    pv = jnp.dot(p.astype(BF16), v_ones, preferred_element_type=F32)
    acc_ref[:, hs] = alpha * acc_ref[:, hs] + pv[:, :HEAD_DIM]
    l_ref[...] = alpha * l_ref[...] + pv[:, HEAD_DIM:]
    m_ref[...] = m_new


def _fox_prompt_kernel(q_ref, k_ref, v_ref, crow_ref, o_ref, qs_s, m_s, l_s, acc_s, *, tq):
    i, j = pl.program_id(1), pl.program_id(2)

    @pl.when(j == 0)
    def _():
        qs_s[...] = (q_ref[...] * (HEAD_DIM ** -0.5)).astype(BF16)
        m_s[...] = jnp.full_like(m_s, M_FLOOR)
        l_s[...] = jnp.zeros_like(l_s)
        acc_s[...] = jnp.zeros_like(acc_s)

    def attend(diagonal):
        causal = _iota((tq, tq), 1) <= _iota((tq, tq), 0)
        for h in range(N_HEADS):
            hs = slice(h * HEAD_DIM, (h + 1) * HEAD_DIM)
            s = _nt(qs_s[:, hs], k_ref[:, hs].astype(BF16)) - crow_ref[0, h:h + 1, :]
            if diagonal:
                s = jnp.where(causal, s, NEG_INF)
            _flash_update(s, v_ref[:, hs], m_s.at[h], l_s.at[h], acc_s, hs)

    @pl.when(j < i)
    def _():
        attend(False)

    @pl.when(j == i)
    def _():
        attend(True)
        for h in range(N_HEADS):
            hs = slice(h * HEAD_DIM, (h + 1) * HEAD_DIM)
            o_ref[:, hs] = acc_s[:, hs] / l_s[h]


def _fox_prompt(q, k, v, crow, nseq, tq=512):
    n = q.shape[0]
    nq = n // nseq // tq
    qspec = pl.BlockSpec((tq, D_MODEL), lambda b, i, j: (b * nq + i, 0))
    kspec = pl.BlockSpec((tq, D_MODEL), lambda b, i, j: (b * nq + jnp.minimum(j, i), 0))
    return pl.pallas_call(
        functools.partial(_fox_prompt_kernel, tq=tq),
        grid=(nseq, nq, nq),
        in_specs=[qspec, kspec, kspec,
                  pl.BlockSpec((1, SUBLANES, tq), lambda b, i, j: (b, 0, jnp.minimum(j, i)))],
        out_specs=qspec,
        out_shape=jax.ShapeDtypeStruct((n, D_MODEL), F32),
        scratch_shapes=[pltpu.VMEM((tq, D_MODEL), BF16),
                        pltpu.VMEM((N_HEADS, tq, LANES), F32), pltpu.VMEM((N_HEADS, tq, LANES), F32),
                        pltpu.VMEM((tq, D_MODEL), F32)],
        compiler_params=_params("parallel", "parallel", "arbitrary"),
        name="fox_prompt",
    )(q, k, v, crow)


HT_ROWS = N_HEADS * SUBLANES
PAGE_ROWS = LANES * N_HEADS


def _stack_heads(x8):
    return jnp.concatenate([x8[:, h * HEAD_DIM:(h + 1) * HEAD_DIM] for h in range(N_HEADS)], axis=0)


def _unstack_heads(x):
    return jnp.concatenate([x[h * SUBLANES:(h + 1) * SUBLANES, :] for h in range(N_HEADS)], axis=1)


def _same_head_past():
    return ((_iota((HT_ROWS, PAGE_ROWS), 1) & (N_HEADS - 1))
            == (_iota((HT_ROWS, PAGE_ROWS), 0) // SUBLANES))


def _visible_new():
    row, lane = _iota((HT_ROWS, LANES), 0), _iota((HT_ROWS, LANES), 1)
    return (((lane // SUBLANES) == (row // SUBLANES))
            & ((lane & (SUBLANES - 1)) <= (row & (SUBLANES - 1))))


def _cumsum_keys(x):
    npg, w = x.shape
    lane, row = _iota(x.shape, 1), _iota(x.shape, 0)
    sh = N_HEADS
    while sh < w:
        r = pltpu.roll(x, sh, 1)
        from_prev_page = jnp.where(row >= 1, pltpu.roll(r, 1, 0), 0.0)
        x = x + jnp.where(lane >= sh, r, from_prev_page)
        sh *= 2
    sh = 1
    while sh < npg:
        x = x + jnp.where(row >= sh, pltpu.roll(x, sh, 0), 0.0)
        sh *= 2
    return x


def _fox_sample_kernel(pt_ref, q_ref, kn_ref, vn_ref, f_ref, bf_ref, lftab_ref, *rest, group):
    kc_refs, vc_refs = rest[:group], rest[group:2 * group]
    o_ref, lf_ref, q_s, m_s, l_s, acc_s, c_s = rest[2 * group:]
    b, p = pl.program_id(0), pl.program_id(1)
    npages = c_s.shape[0]
    one_head = slice(0, HEAD_DIM)

    @pl.when(p == 0)
    def _():
        q_s[...] = (_stack_heads(q_ref[...]) * (HEAD_DIM ** -0.5)).astype(BF16)
        m_s[...] = jnp.full_like(m_s, M_FLOOR)
        l_s[...] = jnp.zeros_like(l_s)
        acc_s[...] = jnp.zeros_like(acc_s)
        for pg in range(npages):
            c_s[pg:pg + 1, :] = lftab_ref[pl.ds(pt_ref[b, pg], 1), :]
        c_s[...] = _cumsum_keys(c_s[...])

    same_head = _same_head_past()
    q = q_s[...]
    s_tiles = [jnp.where(same_head,
                         _nt(q, kc_refs[g][0, 0].astype(BF16)) - c_s[pl.ds(p * group + g, 1), :],
                         NEG_INF) for g in range(group)]
    _online_update(s_tiles, [vc_refs[g][0, 0].astype(BF16) for g in range(group)],
                   m_s, l_s, acc_s, one_head)

    @pl.when(p == pl.num_programs(1) - 1)
    def _():
        lane = _iota((1, LANES), 1)
        lf = jax.nn.log_sigmoid(f_ref[0] + bf_ref[...])
        lf_ref[0] = lf
        cum = lf
        for sh in (1, 2, 4):
            cum = cum + jnp.where((lane & (SUBLANES - 1)) >= sh, pltpu.roll(cum, sh, 1), 0.0)
        tail = c_s[npages - 1:npages, PAGE_ROWS - LANES:]
        spread = ((_iota((LANES, LANES), 0) - (LANES - N_HEADS))
                  == (_iota((LANES, LANES), 1) // SUBLANES)).astype(F32)
        past = jnp.dot(jnp.broadcast_to(tail, (SUBLANES, LANES)), spread, precision=HI,
                       preferred_element_type=F32)[0:1]
        sn = _nt(q_s[...], _pad_rows(_stack_heads(kn_ref[...])).astype(BF16)) - (past + cum)
        sn = jnp.where(_visible_new(), sn, NEG_INF)
        _online_update(sn, _pad_rows(_stack_heads(vn_ref[...])).astype(BF16), m_s, l_s, acc_s, one_head)
        o_ref[...] = _unstack_heads(acc_s[...] / l_s[...])


PAGE_GROUP = 8


def _page_specs(slot, group):
    return [pl.BlockSpec((1, 1, PAGE_ROWS, HEAD_DIM),
                         lambda b, p, pt, g=g: (slot, pt[b, p * group + g], 0, 0))
            for g in range(group)]


def _fox_sample(page_table, q, kn, vn, fflat, bflat, lftab, kc, vc, slot):
    n = q.shape[0]
    nb, npages = page_table.shape
    group = PAGE_GROUP
    assert npages % group == 0

    def rows(w):
        return pl.BlockSpec((SUBLANES, w), lambda b, p, pt: (b, 0))

    pages = _page_specs(slot, group)
    flat = pl.BlockSpec((1, 1, LANES), lambda b, p, pt: (b, 0, 0))
    grid_spec = pltpu.PrefetchScalarGridSpec(
        num_scalar_prefetch=1,
        grid=(nb, npages // group),
        in_specs=[rows(D_MODEL), rows(D_MODEL), rows(D_MODEL), flat,
                  _resident((1, LANES)), _resident(lftab.shape)] + pages + pages,
        out_specs=[rows(D_MODEL), flat],
        scratch_shapes=[pltpu.VMEM((HT_ROWS, HEAD_DIM), BF16),
                        pltpu.VMEM((HT_ROWS, 1), F32), pltpu.VMEM((HT_ROWS, 1), F32),
                        pltpu.VMEM((HT_ROWS, HEAD_DIM), F32),
                        pltpu.VMEM((npages, PAGE_ROWS), F32)])
    return pl.pallas_call(
        functools.partial(_fox_sample_kernel, group=group),
        grid_spec=grid_spec,
        out_shape=[jax.ShapeDtypeStruct((n, D_MODEL), F32), jax.ShapeDtypeStruct((nb, 1, LANES), F32)],
        compiler_params=_params("parallel", "arbitrary"),
        name="fox_sample",
    )(page_table, q, kn, vn, fflat, bflat, lftab, *([kc] * group), *([vc] * group))


def _rope_kernel(q_ref, k_ref, cos_ref, sin_ref, qo_ref, ko_ref, km_ref):
    cos, sin = cos_ref[...], sin_ref[...]
    for h in range(N_HEADS):
        hs = slice(h * HEAD_DIM, (h + 1) * HEAD_DIM)
        for src, dst in ((q_ref, qo_ref), (k_ref, ko_ref)):
            x = src[:, hs]
            dst[:, hs] = x * cos + pltpu.roll(x, HEAD_DIM // 2, 1) * sin
    km_ref[0] = jnp.mean(ko_ref[...], axis=0, keepdims=True)


def _rope(q, k, cos, sin, tt=MOBA_BLOCK):
    n = q.shape[0]
    tt = min(tt, n)
    ntab = cos.shape[0] // tt
    tok = pl.BlockSpec((tt, D_MODEL), lambda i: (i, 0))
    tab = pl.BlockSpec((tt, HEAD_DIM), lambda i: (i % ntab, 0))
    return pl.pallas_call(
        _rope_kernel,
        grid=(n // tt,),
        in_specs=[tok, tok, tab, tab],
        out_specs=[tok, tok, pl.BlockSpec((1, 1, D_MODEL), lambda i: (i, 0, 0))],
        out_shape=[jax.ShapeDtypeStruct((n, D_MODEL), F32), jax.ShapeDtypeStruct((n, D_MODEL), F32),
                   jax.ShapeDtypeStruct((n // tt, 1, D_MODEL), F32)],
        compiler_params=_params("parallel"),
        name="rope",
    )(q, k, cos, sin)


def _top_blocks(gate, n_valid):
    lane = _iota(gate.shape, 1)
    g = jnp.where(lane < n_valid, gate, NEG_INF)
    sel = jnp.zeros(gate.shape, F32)
    for _ in range(MOBA_TOPK):
        mx = jnp.max(g, axis=-1, keepdims=True)
        idx = jnp.min(jnp.where(g == mx, lane, LANES), axis=-1, keepdims=True)
        hit = lane == idx
        sel = jnp.where(hit & (mx > NEG_INF), 1.0, sel)
        g = jnp.where(hit, NEG_INF, g)
    return sel


def _top_blocks_t(gate_t, n_valid):
    nb = gate_t.shape[0]
    blk = _iota(gate_t.shape, 0)
    g = jnp.where(blk < n_valid, gate_t, NEG_INF)
    sel = jnp.zeros(gate_t.shape, F32)
    for _ in range(MOBA_TOPK):
        mx = jnp.max(g, axis=0, keepdims=True)
        idx = jnp.min(jnp.where(g == mx, blk, nb), axis=0, keepdims=True)
        hit = blk == idx
        sel = jnp.where(hit & (mx > NEG_INF), 1.0, sel)
        g = jnp.where(hit, NEG_INF, g)
    return sel


MOBA_TILE = 2 * MOBA_BLOCK


def _moba_prompt_kernel(q_ref, k_ref, v_ref, km_ref, o_ref, qs_s, sel_s, m_s, l_s, acc_s):
    i, j = pl.program_id(1), pl.program_id(2)
    tq = MOBA_TILE
    blocks_per_tile = tq // MOBA_BLOCK

    @pl.when(j == 0)
    def _():
        qs_s[...] = (q_ref[...] * (HEAD_DIM ** -0.5)).astype(BF16)
        m_s[...] = jnp.full_like(m_s, M_FLOOR)
        l_s[...] = jnp.zeros_like(l_s)
        acc_s[...] = jnp.zeros_like(acc_s)
        own = i * blocks_per_tile + _iota((1, tq), 1) // MOBA_BLOCK
        for h in range(N_HEADS):
            hs = slice(h * HEAD_DIM, (h + 1) * HEAD_DIM)
            sel_t = _top_blocks_t(_nt(km_ref[0, :, hs], q_ref[:, hs], HI), own)
            sel_s[h] = _pad_rows(sel_t).T.astype(BF16)

    def attend(diagonal):
        width = blocks_per_tile * LANES
        spread = (_iota((LANES, width), 0)
                  == j * blocks_per_tile + _iota((LANES, width), 1) // LANES).astype(BF16)
        if diagonal:
            row, col = _iota((tq, tq), 0), _iota((tq, tq), 1)
            own_causal = ((row // MOBA_BLOCK) == (col // MOBA_BLOCK)) & (col <= row)
        for h in range(N_HEADS):
            hs = slice(h * HEAD_DIM, (h + 1) * HEAD_DIM)
            picked = jnp.dot(sel_s[h], spread, preferred_element_type=F32)
            visible = jnp.concatenate(
                [picked[:, b * LANES:(b + 1) * LANES] for b in range(blocks_per_tile)
                 for _ in range(MOBA_BLOCK // LANES)], axis=1) > 0.5
            if diagonal:
                visible = visible | own_causal
            s = jnp.where(visible, _nt(qs_s[:, hs], k_ref[:, hs].astype(BF16)), NEG_INF)
            _flash_update(s, v_ref[:, hs], m_s.at[h], l_s.at[h], acc_s, hs)

    @pl.when(j < i)
    def _():
        attend(False)

    @pl.when(j == i)
    def _():
        attend(True)
        for h in range(N_HEADS):
            hs = slice(h * HEAD_DIM, (h + 1) * HEAD_DIM)
            o_ref[:, hs] = acc_s[:, hs] / l_s[h]


def _moba_prompt(q, k, v, kmean, nseq):
    n = q.shape[0]
    tq = MOBA_TILE
    nq = n // nseq // tq
    assert kmean.shape[1] % SUBLANES == 0 and kmean.shape[1] <= LANES
    qspec = pl.BlockSpec((tq, D_MODEL), lambda b, i, j: (b * nq + i, 0))
    kspec = pl.BlockSpec((tq, D_MODEL), lambda b, i, j: (b * nq + jnp.minimum(j, i), 0))
    return pl.pallas_call(
        _moba_prompt_kernel,
        grid=(nseq, nq, nq),
        in_specs=[qspec, kspec, kspec,
                  pl.BlockSpec((1, kmean.shape[1], D_MODEL), lambda b, i, j: (b, 0, 0))],
        out_specs=qspec,
        out_shape=jax.ShapeDtypeStruct((n, D_MODEL), F32),
        scratch_shapes=[pltpu.VMEM((tq, D_MODEL), BF16), pltpu.VMEM((N_HEADS, tq, LANES), BF16),
                        pltpu.VMEM((N_HEADS, tq, LANES), F32), pltpu.VMEM((N_HEADS, tq, LANES), F32),
                        pltpu.VMEM((tq, D_MODEL), F32)],
        compiler_params=_params("parallel", "parallel", "arbitrary"),
        name="moba_prompt",
    )(q, k, v, kmean)


def _moba_sample_kernel(pt_ref, q_ref, kn_ref, vn_ref, *rest, group, pages_per_block):
    kc_refs, vc_refs = rest[:group], rest[group:2 * group]
    o_ref, qf_s, qb_s, m_all, l_all, acc_all, ksum_s = rest[2 * group:]
    p = pl.program_id(1)
    n_blocks = acc_all.shape[0]
    blocks_per_step = group // pages_per_block

    @pl.when(p == 0)
    def _():
        q = _stack_heads(q_ref[...])
        qf_s[...] = q
        qb_s[...] = (q * (HEAD_DIM ** -0.5)).astype(BF16)
        ksum_s[...] = jnp.zeros_like(ksum_s)

    same_head = _same_head_past()
    q = qb_s[...]
    for bi in range(blocks_per_step):
        blk = p * blocks_per_step + bi
        ksum = jnp.zeros((N_HEADS, HEAD_DIM), F32)
        s_tiles, v_tiles = [], []
        for g in range(bi * pages_per_block, (bi + 1) * pages_per_block):
            kpage = kc_refs[g][0, 0]
            ksum = ksum + jnp.sum(kpage.reshape(LANES, N_HEADS, HEAD_DIM), axis=0)
            s_tiles.append(jnp.where(same_head, _nt(q, kpage.astype(BF16)), NEG_INF))
            v_tiles.append(vc_refs[g][0, 0].astype(BF16))
        ksum_s[pl.ds(pl.multiple_of(blk * N_HEADS, N_HEADS), N_HEADS), :] = ksum
        mp = s_tiles[0].max(axis=-1, keepdims=True)
        for st in s_tiles[1:]:
            mp = jnp.maximum(mp, jnp.max(st, axis=-1, keepdims=True))
        lsum = jnp.zeros((HT_ROWS, 1), F32)
        acc = jnp.zeros((HT_ROWS, HEAD_DIM), F32)
        for st, vt in zip(s_tiles, v_tiles):
            e = jnp.exp(st - mp)
            lsum = lsum + jnp.sum(e, axis=-1, keepdims=True)
            acc = acc + jnp.dot(e.astype(BF16), vt, preferred_element_type=F32)
        m_all[blk] = mp
        l_all[blk] = lsum
        acc_all[blk] = acc

    @pl.when(p == pl.num_programs(1) - 1)
    def _():
        kmean = ksum_s[...] * (1.0 / MOBA_BLOCK)
        g = _nt(qf_s[...], kmean, HI)
        same = (_iota((HT_ROWS, LANES), 1) & (N_HEADS - 1)) == (_iota((HT_ROWS, LANES), 0) // SUBLANES)
        group = ((_iota((LANES, LANES), 0) // N_HEADS) == _iota((LANES, LANES), 1)).astype(F32)
        gate = jnp.dot(jnp.where(same, g, 0.0), group, precision=HI, preferred_element_type=F32)
        sel = _top_blocks(gate, n_blocks)
        so = _nt(qb_s[...], _pad_rows(_stack_heads(kn_ref[...])).astype(BF16))
        so = jnp.where(_visible_new(), so, NEG_INF)
        m_own = jnp.max(so, axis=-1, keepdims=True)
        e_own = jnp.exp(so - m_own)
        l_own = jnp.sum(e_own, axis=-1, keepdims=True)
        acc_own = jnp.dot(e_own.astype(BF16), _pad_rows(_stack_heads(vn_ref[...])).astype(BF16),
                          preferred_element_type=F32)
        picks = [sel[:, n:n + 1] > 0.0 for n in range(n_blocks)]
        m_tot = m_own
        for n, pick in enumerate(picks):
            m_tot = jnp.maximum(m_tot, jnp.where(pick, m_all[n], NEG_INF))
        w_own = jnp.exp(m_own - m_tot)
        acc = w_own * acc_own
        l_tot = w_own * l_own
        for n, pick in enumerate(picks):
            w = jnp.where(pick, jnp.exp(m_all[n] - m_tot), 0.0)
            acc = acc + w * acc_all[n]
            l_tot = l_tot + w * l_all[n]
        o_ref[...] = _unstack_heads(acc / l_tot)


def _moba_sample(page_table, q, kn, vn, kc, vc, slot):
    n = q.shape[0]
    nb, npages = page_table.shape
    pages_per_block = MOBA_BLOCK // (kc.shape[2] // N_HEADS)
    n_blocks = npages // pages_per_block
    group = PAGE_GROUP
    assert n_blocks * N_HEADS <= LANES and group % pages_per_block == 0 and npages % group == 0

    def rows(w):
        return pl.BlockSpec((SUBLANES, w), lambda b, p, pt: (b, 0))

    pages = _page_specs(slot, group)
    grid_spec = pltpu.PrefetchScalarGridSpec(
        num_scalar_prefetch=1,
        grid=(nb, npages // group),
        in_specs=[rows(D_MODEL), rows(D_MODEL), rows(D_MODEL)] + pages + pages,
        out_specs=rows(D_MODEL),
        scratch_shapes=[pltpu.VMEM((HT_ROWS, HEAD_DIM), F32), pltpu.VMEM((HT_ROWS, HEAD_DIM), BF16),
                        pltpu.VMEM((n_blocks, HT_ROWS, 1), F32), pltpu.VMEM((n_blocks, HT_ROWS, 1), F32),
                        pltpu.VMEM((n_blocks, HT_ROWS, HEAD_DIM), F32),
                        pltpu.VMEM((LANES, HEAD_DIM), F32)])
    return pl.pallas_call(
        functools.partial(_moba_sample_kernel, group=group, pages_per_block=pages_per_block),
        grid_spec=grid_spec,
        out_shape=jax.ShapeDtypeStruct((n, D_MODEL), F32),
        compiler_params=_params("parallel", "arbitrary"),
        name="moba_sample",
    )(page_table, q, kn, vn, *([kc] * group), *([vc] * group))


def _pad_lanes(a, width=LANES):
    return jnp.pad(a, [(0, 0)] * (a.ndim - 1) + [(0, width - a.shape[-1])])


def _rope_tables(pos):
    half = HEAD_DIM // 2
    inv = ROPE_THETA ** (-jnp.arange(half, dtype=F32) / half)
    ang = pos.astype(F32)[:, None] * inv[None, :]
    cos, sin = jnp.cos(ang), jnp.sin(ang)
    return jnp.concatenate([cos, cos], axis=-1), jnp.concatenate([-sin, sin], axis=-1)


def kernel(x_prompt, x_sample, cache_fox_k, cache_fox_v, cache_fox_logf, cache_moba_k, cache_moba_v,
           state_mlstm_c, state_mlstm_n, state_mlstm_m, state_ffn_conv, page_table,
           norm_mix_g, norm_ffn_g, norm_final_g,
           mlstm_w_in, mlstm_b_gates, mlstm_norm_g, mlstm_w_out,
           fox_w_in, fox_b_f, fox_w_out, moba_w_in, moba_w_out,
           ffn_w_up, ffn_conv_w, ffn_conv_b, ffn_w_down):
    B, S, D = x_prompt.shape
    Bd, T, _ = x_sample.shape
    H, Dh, dk = N_HEADS, HEAD_DIM, MLSTM_QK
    depth = norm_mix_g.shape[0]
    n_pool, page_rows = cache_fox_k.shape[1], cache_fox_k.shape[2]
    past = page_table.shape[1] * page_rows
    assert T == SUBLANES and D == D_MODEL and page_rows == LANES

    xp = x_prompt.reshape(B * S, D)
    xs = x_sample.reshape(Bd * T, D)
    fkp, fks, fvp, fvs, flp, fls = [], [], [], [], [], []
    mkp, mks, mvp, mvs = [], [], [], []
    acp, acs, anp, ans, amp, ams = [], [], [], [], [], []
    cvp, cvs = [], []
    g_mix = norm_mix_g.reshape(depth, 1, D)
    ffn_params = (norm_ffn_g.reshape(depth, 1, D), ffn_w_up.astype(BF16), ffn_conv_w,
                  ffn_conv_b.reshape(depth, 1, 2 * D_FF), ffn_w_down.astype(BF16))
    ffn_prev = state_ffn_conv.astype(F32).reshape(depth, Bd, (CONV_W - 1) * 2 * D_FF)

    for i in range(depth):
        kind, slot = i % N_MIXERS, i // N_MIXERS
        if kind == 0:
            w = mlstm_w_in[slot]
            wq = w[:, :H * dk] * (dk ** -0.5)
            wk = w[:, H * dk:2 * H * dk]
            rest = w[:, 2 * H * dk:2 * H * dk + 2 * H * Dh]
            wig = _pad_lanes(w[:, 2 * H * dk + 2 * H * Dh:2 * H * dk + 2 * H * Dh + H])
            wfg = _pad_lanes(w[:, 2 * H * dk + 2 * H * Dh + H:])
            pad_heads = lambda a: _pad_lanes(a.reshape(D, H, dk), Dh).reshape(D, H * Dh)
            w_p = jnp.concatenate([pad_heads(wq), pad_heads(wk), rest, wig, wfg], axis=1).astype(BF16)
            w_s = jnp.concatenate([wq, wk, rest, wig, wfg], axis=1).astype(BF16)
            bi = _pad_lanes(mlstm_b_gates[slot][:H].reshape(1, H))
            bfg = _pad_lanes(mlstm_b_gates[slot][H:].reshape(1, H))
            ng = mlstm_norm_g[slot].reshape(1, H * Dh)
            wo = mlstm_w_out[slot].astype(BF16)

            q, k, v, o, ig, fg = _norm_proj(xp, g_mix, i, w_p, [H * Dh, H * Dh, H * Dh, H * Dh, LANES, LANES])
            hp, c_, n_, m_ = _mlstm_prompt(q, k, v, o, ig, fg, bi, bfg, ng, B)
            acp.append(c_); anp.append(n_[:, :, :dk]); amp.append(m_.reshape(B, H))
            xp = _proj_res(hp, wo, xp)

            q, k, v, o, ig, fg = _norm_proj(xs, g_mix, i, w_s, [H * dk, H * dk, H * Dh, H * Dh, LANES, LANES])
            hs_, c_, n_, m_ = _mlstm_sample(q, k, v, o, ig, fg, bi, bfg, ng,
                                            state_mlstm_c[slot], state_mlstm_n[slot], state_mlstm_m[slot])
            acs.append(c_); ans.append(n_.reshape(Bd, H, dk)); ams.append(m_.reshape(Bd, H))
            xs = _proj_res(hs_, wo, xs)
        elif kind == 1:
            w = fox_w_in[slot]
            w_b = jnp.concatenate([w[:, :3 * H * Dh], _pad_lanes(w[:, 3 * H * Dh:])], axis=1).astype(BF16)
            bfg = _pad_lanes(fox_b_f[slot].reshape(1, H))
            wo = fox_w_out[slot].astype(BF16)
            widths = [H * Dh, H * Dh, H * Dh, LANES]

            q, k, v, fraw = _norm_proj(xp, g_mix, i, w_b, widths)
            lf, crow = _fox_gates(fraw, bfg, B)
            op = _fox_prompt(q, k, v, crow, B)
            fkp.append(k.reshape(B, S, H, Dh)); fvp.append(v.reshape(B, S, H, Dh))
            flp.append(lf[:, :H].reshape(B, S, H))
            xp = _proj_res(op, wo, xp)

            q, k, v, fraw = _norm_proj(xs, g_mix, i, w_b, widths)
            fflat = _pad_lanes(fraw[:, :H].reshape(Bd, T, H).transpose(0, 2, 1).reshape(Bd, 1, H * T))
            bflat = _pad_lanes(jnp.repeat(fox_b_f[slot], T).reshape(1, H * T))
            n_slots = cache_fox_k.shape[0]
            os_, lf = _fox_sample(page_table, q, k, v, fflat, bflat,
                                  cache_fox_logf[slot].astype(F32).reshape(n_pool, page_rows * H),
                                  cache_fox_k.astype(F32).reshape(n_slots, n_pool, page_rows * H, Dh),
                                  cache_fox_v.astype(F32).reshape(n_slots, n_pool, page_rows * H, Dh), slot)
            fks.append(k.reshape(Bd, T, H, Dh)); fvs.append(v.reshape(Bd, T, H, Dh))
            fls.append(lf[:, 0, :H * T].reshape(Bd, H, T).transpose(0, 2, 1))
            xs = _proj_res(os_, wo, xs)
        else:
            w_b = moba_w_in[slot].astype(BF16)
            wo = moba_w_out[slot].astype(BF16)
            widths = [H * Dh, H * Dh, H * Dh]
            cos_p, sin_p = _rope_tables(jnp.arange(S, dtype=jnp.int32))
            cos_s, sin_s = _rope_tables(past + jnp.arange(T, dtype=jnp.int32))
            reps = min(MOBA_BLOCK, Bd * T) // T
            cos_s, sin_s = jnp.tile(cos_s, (reps, 1)), jnp.tile(sin_s, (reps, 1))

            q, k, v = _norm_proj(xp, g_mix, i, w_b, widths)
            q, k, km = _rope(q, k, cos_p, sin_p)
            nblk = S // MOBA_BLOCK
            km = jnp.pad(km.reshape(B, nblk, D), ((0, 0), (0, -nblk % SUBLANES), (0, 0)))
            op = _moba_prompt(q, k, v, km, B)
            mkp.append(k.reshape(B, S, H, Dh)); mvp.append(v.reshape(B, S, H, Dh))
            xp = _proj_res(op, wo, xp)

            q, k, v = _norm_proj(xs, g_mix, i, w_b, widths)
            q, k, _ = _rope(q, k, cos_s, sin_s)
            n_slots = cache_moba_k.shape[0]
            os_ = _moba_sample(page_table, q, k, v,
                               cache_moba_k.astype(F32).reshape(n_slots, n_pool, page_rows * H, Dh),
                               cache_moba_v.astype(F32).reshape(n_slots, n_pool, page_rows * H, Dh), slot)
            mks.append(k.reshape(Bd, T, H, Dh)); mvs.append(v.reshape(Bd, T, H, Dh))
            xs = _proj_res(os_, wo, xs)

        xp, tail = _ffn_prompt(xp, ffn_params, i, B)
        cvp.append(tail.reshape(B, SUBLANES, 2 * D_FF)[:, SUBLANES - (CONV_W - 1):])
        xs, new = _ffn_sample(xs, ffn_params, i, ffn_prev)
        cvs.append(new.reshape(Bd, CONV_W - 1, 2 * D_FF))

    y_prompt = _final_norm(xp, norm_final_g).reshape(B, S, D)
    y_sample = _final_norm(xs, norm_final_g).reshape(Bd, T, D)
    return (y_prompt, y_sample,
            jnp.stack(fkp), jnp.stack(fks), jnp.stack(fvp), jnp.stack(fvs), jnp.stack(flp), jnp.stack(fls),
            jnp.stack(mkp), jnp.stack(mks), jnp.stack(mvp), jnp.stack(mvs),
            jnp.stack(acp), jnp.stack(acs), jnp.stack(anp), jnp.stack(ans), jnp.stack(amp), jnp.stack(ams),
            jnp.stack(cvp), jnp.stack(cvs))
```

```python
import functools

import jax
import jax.numpy as jnp
from jax import lax
from jax.experimental import pallas as pl
from jax.experimental.pallas import tpu as pltpu

F32 = jnp.float32
BF16 = jnp.bfloat16
HI = lax.Precision.HIGHEST

D_MODEL = 1024
N_HEADS = 8
HEAD_DIM = 128
MLSTM_QK = 64
D_FF = 2816
CONV_W = 3
N_MIXERS = 3
MOBA_BLOCK = 256
MOBA_TOPK = 3
ROPE_THETA = 10000.0
RMS_EPS = 1e-6
NEG_INF = float("-inf")
M_FLOOR = -1e30
LANES = 128
SUBLANES = 8
VMEM_LIMIT = 56 * 1024 * 1024

NT_DIMS = (((1,), (1,)), ((), ()))
TN_DIMS = (((0,), (0,)), ((), ()))


def _params(*sem):
    return pltpu.CompilerParams(dimension_semantics=sem, vmem_limit_bytes=VMEM_LIMIT)


def _resident(shape):
    nd = len(shape)
    return pl.BlockSpec(shape, lambda *_: (0,) * nd, pipeline_mode=pl.Buffered(1))


def _rms(x, g):
    return x * lax.rsqrt(jnp.mean(x * x, axis=-1, keepdims=True) + RMS_EPS) * g


def _iota(shape, axis):
    return lax.broadcasted_iota(jnp.int32, shape, axis)


def _eye_rows(n):
    return (_iota((n, LANES), 0) == _iota((n, LANES), 1)).astype(F32)


def _nt(a, b, precision=None):
    return lax.dot_general(a, b, NT_DIMS, precision=precision, preferred_element_type=F32)


def _norm_proj_kernel(x_ref, g_ref, w_ref, *o_refs, widths):
    xn = _rms(x_ref[...], g_ref[0]).astype(BF16)
    off = 0
    for o_ref, wd in zip(o_refs, widths):
        for c in range(0, wd, 512):
            cw = min(512, wd - c)
            o_ref[:, c:c + cw] = jnp.dot(xn, w_ref[:, off + c:off + c + cw],
                                         preferred_element_type=F32)
        off += wd


def _norm_proj(x, g, layer, w, widths, tm=256):
    n = x.shape[0]
    tm = min(tm, n)
    assert sum(widths) == w.shape[1] and n % tm == 0
    return pl.pallas_call(
        functools.partial(_norm_proj_kernel, widths=tuple(widths)),
        grid=(n // tm,),
        in_specs=[pl.BlockSpec((tm, D_MODEL), lambda i: (i, 0)),
                  _layer_block(g, layer),
                  _resident(w.shape)],
        out_specs=[pl.BlockSpec((tm, wd), lambda i: (i, 0)) for wd in widths],
        out_shape=[jax.ShapeDtypeStruct((n, wd), F32) for wd in widths],
        compiler_params=_params("parallel"),
        name="norm_proj",
    )(x, g, w)


def _norm_proj_rope_kernel(x_ref, g_ref, w_ref, cos_ref, sin_ref, q_ref, k_ref, v_ref, km_ref):
    xn = _rms(x_ref[...], g_ref[0]).astype(BF16)
    cos, sin = cos_ref[...], sin_ref[...]
    pair = 2 * HEAD_DIM
    for dst, off in ((q_ref, 0), (k_ref, D_MODEL)):
        for c in range(0, D_MODEL, pair):
            t = jnp.dot(xn, w_ref[:, off + c:off + c + pair], preferred_element_type=F32)
            for half in range(2):
                lo = c + half * HEAD_DIM
                xh = t[:, half * HEAD_DIM:(half + 1) * HEAD_DIM]
                rot = xh * cos + pltpu.roll(xh, HEAD_DIM // 2, 1) * sin
                dst[:, lo:lo + HEAD_DIM] = rot
                if dst is k_ref:
                    km_ref[0, :, lo:lo + HEAD_DIM] = jnp.mean(rot, axis=0, keepdims=True)
    for c in range(0, D_MODEL, 512):
        v_ref[:, c:c + 512] = jnp.dot(xn, w_ref[:, 2 * D_MODEL + c:2 * D_MODEL + c + 512],
                                      preferred_element_type=F32)


def _norm_proj_rope(x, g, layer, w, cos, sin, tm=MOBA_BLOCK):
    n = x.shape[0]
    tm = min(tm, n)
    ntab = cos.shape[0] // tm
    tok = pl.BlockSpec((tm, D_MODEL), lambda i: (i, 0))
    tab = pl.BlockSpec((tm, HEAD_DIM), lambda i: (i % ntab, 0))
    return pl.pallas_call(
        _norm_proj_rope_kernel,
        grid=(n // tm,),
        in_specs=[tok, _layer_block(g, layer), _resident(w.shape), tab, tab],
        out_specs=[tok, tok, tok, pl.BlockSpec((1, 1, D_MODEL), lambda i: (i, 0, 0))],
        out_shape=[jax.ShapeDtypeStruct((n, D_MODEL), F32)] * 3
                  + [jax.ShapeDtypeStruct((n // tm, 1, D_MODEL), F32)],
        compiler_params=_params("parallel"),
        name="norm_proj_rope",
    )(x, g, w, cos, sin)


FFN_CH = 256


def _layer_block(arr, layer):
    nd = arr.ndim
    return pl.BlockSpec((1,) + arr.shape[1:], lambda *_: (layer,) + (0,) * (nd - 1),
                        pipeline_mode=pl.Buffered(1))


def _ffn_kernel(*refs, tt, grouped, final):
    x_ref, mix_ref, wo_ref, g_ref, wup_ref, cw_ref, cb_ref, wdn_ref = refs[:8]
    refs = refs[8:]
    if final:
        gf_ref, refs = refs[0], refs[1:]
    if grouped:
        prev_ref, o_ref, new_ref, hbuf, p_s, u_s = refs
        groups = tt // SUBLANES
        p_s[...] = jnp.zeros_like(p_s)
    else:
        o_ref, tail_ref, hbuf, carry = refs

        @pl.when(pl.program_id(1) == 0)
        def _():
            carry[...] = jnp.zeros_like(carry)

    x = x_ref[...] + jnp.dot(mix_ref[...].astype(BF16), wo_ref[...], preferred_element_type=F32)
    xn = _rms(x, g_ref[0]).astype(BF16)
    row = _iota((tt, 1), 0)
    for c in range(D_FF // FFN_CH):
        ys = []
        for part in range(2):
            c0 = part * D_FF + c * FFN_CH
            cols = slice(c0, c0 + FFN_CH)
            u = jnp.dot(xn, wup_ref[0, :, cols], preferred_element_type=F32)
            r1 = pltpu.roll(u, 1, 0)
            r2 = pltpu.roll(u, 2, 0)
            if grouped:
                for k in range(FFN_CH // LANES):
                    lo = c0 + k * LANES
                    p_s[k, pl.ds(0, groups, stride=SUBLANES), :] = prev_ref[0, :, 0, lo:lo + LANES]
                    p_s[k, pl.ds(1, groups, stride=SUBLANES), :] = prev_ref[0, :, 1, lo:lo + LANES]
                    u_s[k] = u[:, k * LANES:(k + 1) * LANES]
                    new_ref[:, 0, lo:lo + LANES] = u_s[k, pl.ds(SUBLANES - 2, groups, stride=SUBLANES), :]
                    new_ref[:, 1, lo:lo + LANES] = u_s[k, pl.ds(SUBLANES - 1, groups, stride=SUBLANES), :]
                p2 = jnp.concatenate([p_s[k] for k in range(FFN_CH // LANES)], axis=1)
                p1 = pltpu.roll(p2, tt - 1, 0)
                t8 = row & (SUBLANES - 1)
                um1 = jnp.where(t8 < 1, p1, r1)
                um2 = jnp.where(t8 < 2, p2, r2)
            else:
                pc = carry[:, cols]
                row8 = row[:SUBLANES]
                f1 = jnp.where(row8 < 1, pltpu.roll(pc, 1, 0), r1[:SUBLANES])
                f2 = jnp.where(row8 < 2, pltpu.roll(pc, 2, 0), r2[:SUBLANES])
                um1 = jnp.concatenate([f1, r1[SUBLANES:]], axis=0)
                um2 = jnp.concatenate([f2, r2[SUBLANES:]], axis=0)
                tail = u[tt - SUBLANES:]
                carry[:, cols] = tail
                tail_ref[:, cols] = tail
            y = cb_ref[0, :, cols] + cw_ref[0, 0:1, cols] * um2
            y = y + cw_ref[0, 1:2, cols] * um1
            y = y + cw_ref[0, 2:3, cols] * u
            ys.append(y)
        gate, val = ys
        hbuf[:, c * FFN_CH:(c + 1) * FFN_CH] = (gate * jax.nn.sigmoid(gate) * val).astype(BF16)
    out = x + jnp.dot(hbuf[...], wdn_ref[0], preferred_element_type=F32)
    o_ref[...] = _rms(out, gf_ref[...]) if final else out


def _ffn_params(wo, g, wup, cw, cb, wdn, layer, final_g):
    specs = [_resident(wo.shape)] + [_layer_block(a, layer) for a in (g, wup, cw, cb, wdn)]
    return specs + ([_resident(final_g.shape)] if final_g is not None else [])


def _ffn_prompt(x, mix, wo, params, layer, nseq, final_g=None, tt=512):
    n = x.shape[0]
    nt = n // nseq // tt
    tok = pl.BlockSpec((tt, D_MODEL), lambda b, t: (b * nt + t, 0))
    extra = () if final_g is None else (final_g,)
    return pl.pallas_call(
        functools.partial(_ffn_kernel, tt=tt, grouped=False, final=final_g is not None),
        grid=(nseq, nt),
        in_specs=[tok, tok] + _ffn_params(wo, *params, layer, final_g),
        out_specs=[tok, pl.BlockSpec((SUBLANES, 2 * D_FF), lambda b, t: (b, 0))],
        out_shape=[jax.ShapeDtypeStruct((n, D_MODEL), F32),
                   jax.ShapeDtypeStruct((nseq * SUBLANES, 2 * D_FF), F32)],
        scratch_shapes=[pltpu.VMEM((tt, D_FF), BF16), pltpu.VMEM((SUBLANES, 2 * D_FF), F32)],
        compiler_params=_params("parallel", "arbitrary"),
        name="ffn_prompt",
    )(x, mix, wo, *params, *extra)


def _ffn_sample(x, mix, wo, params, layer, prev, final_g=None, tt=256):
    n = x.shape[0]
    tt = min(tt, n)
    groups = tt // SUBLANES
    hist = CONV_W - 1
    tok = pl.BlockSpec((tt, D_MODEL), lambda i: (i, 0))
    extra = () if final_g is None else (final_g,)
    return pl.pallas_call(
        functools.partial(_ffn_kernel, tt=tt, grouped=True, final=final_g is not None),
        grid=(n // tt,),
        in_specs=[tok, tok] + _ffn_params(wo, *params, layer, final_g)
                 + [pl.BlockSpec((1, groups, hist, 2 * D_FF), lambda i: (layer, i, 0, 0))],
        out_specs=[tok, pl.BlockSpec((groups, hist, 2 * D_FF), lambda i: (i, 0, 0))],
        out_shape=[jax.ShapeDtypeStruct((n, D_MODEL), F32),
                   jax.ShapeDtypeStruct((n // SUBLANES, hist, 2 * D_FF), F32)],
        scratch_shapes=[pltpu.VMEM((tt, D_FF), BF16), pltpu.VMEM((FFN_CH // LANES, tt, LANES), F32),
                        pltpu.VMEM((FFN_CH // LANES, tt, LANES), F32)],
        compiler_params=_params("parallel"),
        name="ffn_sample",
    )(x, mix, wo, *params, *extra, prev)


def _mlstm_prompt_kernel(q_ref, k_ref, v_ref, o_ref, ig_ref, fg_ref, bi_ref, bf_ref, ng_ref,
                         h_ref, cout_ref, nout_ref, mout_ref, c_s, m_s, *, chunk):
    L = chunk
    j = pl.program_id(1)

    def tile(x, n):
        return jnp.concatenate([x] * n, axis=1)

    @pl.when(j == 0)
    def _():
        c_s[...] = jnp.zeros_like(c_s)
        m_s[...] = jnp.zeros_like(m_s)

    ig = ig_ref[...] + bi_ref[...]
    lf = jax.nn.log_sigmoid(fg_ref[...] + bf_ref[...])
    tril = (_iota((L, L), 0) >= _iota((L, L), 1))
    bt = jnp.dot(tril.astype(F32), lf, precision=HI, preferred_element_type=F32)
    rc = ig - bt
    r_t = _nt(_eye_rows(SUBLANES), rc, HI)
    mean_mat = jnp.full((HEAD_DIM, HEAD_DIM), 1.0 / HEAD_DIM, F32)
    ones_b = jnp.ones((L, LANES), BF16)
    for h in range(N_HEADS):
        hs = slice(h * HEAD_DIM, (h + 1) * HEAD_DIM)
        kh = k_ref[:, hs]
        qb, kb = q_ref[:, hs].astype(BF16), kh.astype(BF16)
        v_ones = jnp.concatenate([v_ref[:, hs].astype(BF16), ones_b], axis=1)
        bcol = jnp.broadcast_to(bt[:, h:h + 1], (L, LANES))
        rcol = jnp.broadcast_to(rc[:, h:h + 1], (L, LANES))
        dm = jnp.where(tril, tile(bcol, L // LANES) + r_t[h:h + 1, :], NEG_INF)
        m_prev = m_s[h]
        inter = bcol + m_prev
        mt = jnp.maximum(jnp.max(dm, axis=-1, keepdims=True), inter)
        w_intra = jnp.exp(dm - tile(mt, L // LANES))
        w_inter = jnp.exp(inter - mt)
        c_prev = c_s[h]
        s = _nt(qb, kb) * w_intra
        tot = (jnp.dot(s.astype(BF16), v_ones, preferred_element_type=F32)
               + jnp.dot(qb, c_prev.astype(BF16), preferred_element_type=F32) * tile(w_inter, 2))
        den = jnp.maximum(jnp.abs(tot[:, HEAD_DIM:]), jnp.exp(-mt))
        hh = tot[:, :HEAD_DIM] / den
        ms = jnp.dot(hh * hh, mean_mat, precision=HI, preferred_element_type=F32)
        hh = hh * lax.rsqrt(ms + RMS_EPS) * ng_ref[:, hs]
        h_ref[:, hs] = hh * jax.nn.sigmoid(o_ref[:, hs])
        b_last = bcol[L - 1:L, :]
        gcol = b_last + rcol
        m_new = jnp.maximum(b_last + m_prev, jnp.max(gcol, axis=0, keepdims=True))
        a_prev = jnp.exp(b_last + m_prev - m_new)
        ka = kh * jnp.exp(gcol - m_new)
        c_s[h] = tile(a_prev, 2) * c_prev + lax.dot_general(ka.astype(BF16), v_ones, TN_DIMS,
                                                            preferred_element_type=F32)
        m_s[h] = m_new

    @pl.when(j == pl.num_programs(1) - 1)
    def _():
        for h in range(N_HEADS):
            cout_ref[0, h] = c_s[h][:MLSTM_QK, :HEAD_DIM]
            nout_ref[0, h:h + 1, :] = c_s[h][:, HEAD_DIM:].T[0:1, :]
            mout_ref[0, :, h:h + 1] = m_s[h][:, 0:1]


def _mlstm_prompt(q, k, v, o, ig, fg, bi, bfg, ng, nseq, chunk=256):
    n = q.shape[0]
    nc = n // nseq // chunk
    wide = pl.BlockSpec((chunk, D_MODEL), lambda b, j: (b * nc + j, 0))
    narrow = pl.BlockSpec((chunk, LANES), lambda b, j: (b * nc + j, 0))
    return pl.pallas_call(
        functools.partial(_mlstm_prompt_kernel, chunk=chunk),
        grid=(nseq, nc),
        in_specs=[wide, wide, wide, wide, narrow, narrow,
                  _resident((1, LANES)), _resident((1, LANES)), _resident((1, D_MODEL))],
        out_specs=[wide,
                   pl.BlockSpec((1, N_HEADS, MLSTM_QK, HEAD_DIM), lambda b, j: (b, 0, 0, 0)),
                   pl.BlockSpec((1, N_HEADS, LANES), lambda b, j: (b, 0, 0)),
                   pl.BlockSpec((1, 1, N_HEADS), lambda b, j: (b, 0, 0))],
        out_shape=[jax.ShapeDtypeStruct((n, D_MODEL), F32),
                   jax.ShapeDtypeStruct((nseq, N_HEADS, MLSTM_QK, HEAD_DIM), F32),
                   jax.ShapeDtypeStruct((nseq, N_HEADS, LANES), F32),
                   jax.ShapeDtypeStruct((nseq, 1, N_HEADS), F32)],
        scratch_shapes=[pltpu.VMEM((N_HEADS, HEAD_DIM, 2 * HEAD_DIM), F32),
                        pltpu.VMEM((N_HEADS, 1, LANES), F32)],
        compiler_params=_params("parallel", "arbitrary"),
        name="mlstm_prompt",
    )(q, k, v, o, ig, fg, bi, bfg, ng)


def _cumsum_rows8(x):
    row = _iota(x.shape, 0)
    for sh in (1, 2, 4):
        x = x + jnp.where(row >= sh, pltpu.roll(x, sh, 0), 0.0)
    return x


def _stack_cols(x, n=N_HEADS):
    return jnp.concatenate([x[:, h:h + 1] for h in range(n)], axis=0)


def _stack_bcast(x, rows, n=N_HEADS):
    return jnp.concatenate([jnp.broadcast_to(x[:, h:h + 1], (rows, 1)) for h in range(n)], axis=0)


def _pad_rows(x, rows=LANES):
    return jnp.concatenate([x, jnp.zeros((rows - x.shape[0], x.shape[1]), x.dtype)], axis=0)


def _mlstm_sample_kernel(q_ref, k_ref, v_ref, o_ref, ig_ref, fg_ref, bi_ref, bf_ref, ng_ref,
                         c0_ref, n0_ref, m0_ref, h_ref, cout_ref, nout_ref, mout_ref):
    T, HT, QW = SUBLANES, N_HEADS * SUBLANES, N_HEADS * MLSTM_QK
    ig = ig_ref[...] + bi_ref[...]
    lf = jax.nn.log_sigmoid(fg_ref[...] + bf_ref[...])
    bt = _cumsum_rows8(lf)
    rc = ig - bt
    m0 = m0_ref[0]
    bcol = _stack_cols(bt)
    r_t = _nt(_eye_rows(SUBLANES), _pad_rows(rc), HI)
    rrow = jnp.concatenate([jnp.broadcast_to(r_t[h:h + 1, :], (T, LANES))
                            for h in range(N_HEADS)], axis=0)
    lane = _iota((HT, LANES), 1)
    tok = _iota((HT, LANES), 0) & (T - 1)
    dm = jnp.where(lane <= tok, bcol + rrow, NEG_INF)
    inter = bcol + _stack_bcast(m0, T)
    mt = jnp.maximum(jnp.max(dm, axis=-1, keepdims=True), inter)
    w_intra = jnp.exp(dm - mt)
    w_inter = jnp.exp(inter - mt)

    head_of_row = _iota((HT, QW), 0) // T
    head_of_lane = _iota((HT, QW), 1) // MLSTM_QK
    diag = head_of_row == head_of_lane
    q8, k8, v8 = q_ref[...], k_ref[...], v_ref[...]
    qbd = jnp.where(diag, jnp.concatenate([q8] * N_HEADS, axis=0), 0.0)
    qbd_b = qbd.astype(BF16)
    s = _nt(qbd_b, _pad_rows(k8).astype(BF16)) * w_intra
    nv = jnp.dot(s.astype(BF16), _pad_rows(v8).astype(BF16), preferred_element_type=F32)
    row_head = _iota((HT, HEAD_DIM), 0) // T
    num = jnp.zeros((HT, HEAD_DIM), F32)
    for h in range(N_HEADS):
        num = num + jnp.where(row_head == h, nv[:, h * HEAD_DIM:(h + 1) * HEAD_DIM], 0.0)
    c_prev = c0_ref[0].reshape(QW, HEAD_DIM)
    n_prev = n0_ref[0]
    num = num + jnp.dot(qbd_b, c_prev.astype(BF16), preferred_element_type=F32) * w_inter
    den = (jnp.sum(s, axis=-1, keepdims=True)
           + jnp.sum(qbd * n_prev, axis=-1, keepdims=True) * w_inter)
    den = jnp.maximum(jnp.abs(den), jnp.exp(-mt))
    hh = num / den
    ng = jnp.concatenate([jnp.broadcast_to(ng_ref[:, h * HEAD_DIM:(h + 1) * HEAD_DIM], (T, HEAD_DIM))
                          for h in range(N_HEADS)], axis=0)
    o8 = o_ref[...]
    ost = jnp.concatenate([o8[:, h * HEAD_DIM:(h + 1) * HEAD_DIM] for h in range(N_HEADS)], axis=0)
    hh = hh * lax.rsqrt(jnp.mean(hh * hh, axis=-1, keepdims=True) + RMS_EPS) * ng
    hh = hh * jax.nn.sigmoid(ost)
    h_ref[...] = jnp.concatenate([hh[h * T:(h + 1) * T, :] for h in range(N_HEADS)], axis=1)

    b_last = bt[T - 1:T, :]
    g2 = b_last + rc
    m_new = jnp.maximum(b_last + m0, jnp.max(g2, axis=0, keepdims=True))
    a_prev = jnp.exp(b_last + m0 - m_new)
    a_tok = jnp.exp(g2 - m_new)
    kabd = jnp.where(diag, jnp.concatenate([k8] * N_HEADS, axis=0) * _stack_cols(a_tok), 0.0)
    vst = jnp.concatenate([v8[:, h * HEAD_DIM:(h + 1) * HEAD_DIM] for h in range(N_HEADS)], axis=0)
    dc = lax.dot_general(kabd.astype(BF16), vst.astype(BF16), TN_DIMS, preferred_element_type=F32)
    c_new = _stack_bcast(a_prev, MLSTM_QK) * c_prev + dc
    cout_ref[0] = c_new.reshape(N_HEADS, MLSTM_QK, HEAD_DIM)
    a_lane = jnp.max(jnp.where(diag, _stack_bcast(a_prev, T), 0.0), axis=0, keepdims=True)
    nout_ref[0] = a_lane * n_prev + jnp.sum(kabd, axis=0, keepdims=True)
    mout_ref[0] = m_new[:, :N_HEADS]


def _mlstm_sample(q, k, v, o, ig, fg, bi, bfg, ng, c0, n0, m0):
    n = q.shape[0]
    nb = n // SUBLANES
    qw = N_HEADS * MLSTM_QK

    def rows(w):
        return pl.BlockSpec((SUBLANES, w), lambda b: (b, 0))

    cspec = pl.BlockSpec((1, N_HEADS, MLSTM_QK, HEAD_DIM), lambda b: (b, 0, 0, 0))
    nspec = pl.BlockSpec((1, 1, qw), lambda b: (b, 0, 0))
    mspec = pl.BlockSpec((1, 1, N_HEADS), lambda b: (b, 0, 0))
    return pl.pallas_call(
        _mlstm_sample_kernel,
        grid=(nb,),
        in_specs=[rows(qw), rows(qw), rows(D_MODEL), rows(D_MODEL), rows(LANES), rows(LANES),
                  _resident((1, LANES)), _resident((1, LANES)), _resident((1, D_MODEL)),
                  cspec, nspec, pl.BlockSpec((1, 1, LANES), lambda b: (b, 0, 0))],
        out_specs=[rows(D_MODEL), cspec, nspec, mspec],
        out_shape=[jax.ShapeDtypeStruct((n, D_MODEL), F32),
                   jax.ShapeDtypeStruct((nb, N_HEADS, MLSTM_QK, HEAD_DIM), F32),
                   jax.ShapeDtypeStruct((nb, 1, qw), F32),
                   jax.ShapeDtypeStruct((nb, 1, N_HEADS), F32)],
        compiler_params=_params("parallel"),
        name="mlstm_sample",
    )(q, k, v, o, ig, fg, bi, bfg, ng, c0, n0.reshape(nb, 1, qw), _pad_lanes(m0).reshape(nb, 1, LANES))


def _fox_gate_kernel(f_ref, bf_ref, lf_ref, crow_ref, carry, *, ts):
    @pl.when(pl.program_id(1) == 0)
    def _():
        carry[...] = jnp.zeros_like(carry)

    lf = jax.nn.log_sigmoid(f_ref[...] + bf_ref[...])
    tril = (_iota((ts, ts), 0) >= _iota((ts, ts), 1)).astype(F32)
    c = jnp.dot(tril, lf, precision=HI, preferred_element_type=F32) + carry[...]
    carry[...] = c[ts - 1:ts, :]
    lf_ref[...] = lf
    crow_ref[0] = _nt(_eye_rows(SUBLANES), c, HI)


def _fox_gates(fraw, bfg, nseq, ts=512):
    n = fraw.shape[0]
    seq = n // nseq
    nt = seq // ts
    tok = pl.BlockSpec((ts, LANES), lambda b, t: (b * nt + t, 0))
    return pl.pallas_call(
        functools.partial(_fox_gate_kernel, ts=ts),
        grid=(nseq, nt),
        in_specs=[tok, _resident((1, LANES))],
        out_specs=[tok, pl.BlockSpec((1, SUBLANES, ts), lambda b, t: (b, 0, t))],
        out_shape=[jax.ShapeDtypeStruct((n, LANES), F32),
                   jax.ShapeDtypeStruct((nseq, SUBLANES, seq), F32)],
        scratch_shapes=[pltpu.VMEM((1, LANES), F32)],
        compiler_params=_params("parallel", "arbitrary"),
        name="fox_gates",
    )(fraw, bfg)


def _online_update(s, vb, m_ref, l_ref, acc_ref, hs):
    s_tiles, v_tiles = (s, vb) if isinstance(s, (list, tuple)) else ([s], [vb])
    m_prev = m_ref[...]
    m_new = m_prev
    for st in s_tiles:
        m_new = jnp.maximum(m_new, jnp.max(st, axis=-1, keepdims=True))
    alpha = jnp.exp(m_prev - m_new)
    l_new = alpha * l_ref[...]
    acc = alpha * acc_ref[:, hs]
    for st, vt in zip(s_tiles, v_tiles):
        p = jnp.exp(st - m_new)
        l_new = l_new + jnp.sum(p, axis=-1, keepdims=True)
        acc = acc + jnp.dot(p.astype(BF16), vt, preferred_element_type=F32)
    l_ref[...] = l_new
    acc_ref[:, hs] = acc
    m_ref[...] = m_new


def _flash_update(s, v, m_ref, l_ref, acc_ref, hs):
    m_prev = m_ref[...]
    m_new = jnp.maximum(m_prev, jnp.max(s, axis=-1, keepdims=True))
    alpha = jnp.exp(m_prev - m_new)
    p = jnp.exp(s - jnp.concatenate([m_new] * (s.shape[1] // LANES), axis=1))
    v_ones = jnp.concatenate([v.astype(BF16), jnp.ones((v.shape[0], LANES), BF16)], axis=1)
    pv = jnp.dot(p.astype(BF16), v_ones, preferred_element_type=F32)
    acc_ref[:, hs] = alpha * acc_ref[:, hs] + pv[:, :HEAD_DIM]
    l_ref[...] = alpha * l_ref[...] + pv[:, HEAD_DIM:]
    m_ref[...] = m_new


def _fox_prompt_kernel(q_ref, k_ref, v_ref, crow_ref, o_ref, qs_s, m_s, l_s, acc_s, *, tq):
    i, j = pl.program_id(1), pl.program_id(2)

    @pl.when(j == 0)
    def _():
        qs_s[...] = (q_ref[...] * (HEAD_DIM ** -0.5)).astype(BF16)
        m_s[...] = jnp.full_like(m_s, M_FLOOR)
        l_s[...] = jnp.zeros_like(l_s)
        acc_s[...] = jnp.zeros_like(acc_s)

    def attend(diagonal):
        causal = _iota((tq, tq), 1) <= _iota((tq, tq), 0)
        for h in range(N_HEADS):
            hs = slice(h * HEAD_DIM, (h + 1) * HEAD_DIM)
            s = _nt(qs_s[:, hs], k_ref[:, hs].astype(BF16)) - crow_ref[0, h:h + 1, :]
            if diagonal:
                s = jnp.where(causal, s, NEG_INF)
            _flash_update(s, v_ref[:, hs], m_s.at[h], l_s.at[h], acc_s, hs)

    @pl.when(j < i)
    def _():
        attend(False)

    @pl.when(j == i)
    def _():
        attend(True)
        for h in range(N_HEADS):
            hs = slice(h * HEAD_DIM, (h + 1) * HEAD_DIM)
            o_ref[:, hs] = acc_s[:, hs] / l_s[h]


def _fox_prompt(q, k, v, crow, nseq, tq=512):
    n = q.shape[0]
    nq = n // nseq // tq
    qspec = pl.BlockSpec((tq, D_MODEL), lambda b, i, j: (b * nq + i, 0))
    kspec = pl.BlockSpec((tq, D_MODEL), lambda b, i, j: (b * nq + jnp.minimum(j, i), 0))
    return pl.pallas_call(
        functools.partial(_fox_prompt_kernel, tq=tq),
        grid=(nseq, nq, nq),
        in_specs=[qspec, kspec, kspec,
                  pl.BlockSpec((1, SUBLANES, tq), lambda b, i, j: (b, 0, jnp.minimum(j, i)))],
        out_specs=qspec,
        out_shape=jax.ShapeDtypeStruct((n, D_MODEL), F32),
        scratch_shapes=[pltpu.VMEM((tq, D_MODEL), BF16),
                        pltpu.VMEM((N_HEADS, tq, LANES), F32), pltpu.VMEM((N_HEADS, tq, LANES), F32),
                        pltpu.VMEM((tq, D_MODEL), F32)],
        compiler_params=_params("parallel", "parallel", "arbitrary"),
        name="fox_prompt",
    )(q, k, v, crow)


HT_ROWS = N_HEADS * SUBLANES
PAGE_ROWS = LANES * N_HEADS


def _stack_heads(x8):
    return jnp.concatenate([x8[:, h * HEAD_DIM:(h + 1) * HEAD_DIM] for h in range(N_HEADS)], axis=0)


def _unstack_heads(x):
    return jnp.concatenate([x[h * SUBLANES:(h + 1) * SUBLANES, :] for h in range(N_HEADS)], axis=1)


def _same_head_past():
    return ((_iota((HT_ROWS, PAGE_ROWS), 1) & (N_HEADS - 1))
            == (_iota((HT_ROWS, PAGE_ROWS), 0) // SUBLANES))


def _visible_new():
    row, lane = _iota((HT_ROWS, LANES), 0), _iota((HT_ROWS, LANES), 1)
    return (((lane // SUBLANES) == (row // SUBLANES))
            & ((lane & (SUBLANES - 1)) <= (row & (SUBLANES - 1))))


def _cumsum_keys(x):
    npg, w = x.shape
    lane, row = _iota(x.shape, 1), _iota(x.shape, 0)
    sh = N_HEADS
    while sh < w:
        r = pltpu.roll(x, sh, 1)
        from_prev_page = jnp.where(row >= 1, pltpu.roll(r, 1, 0), 0.0)
        x = x + jnp.where(lane >= sh, r, from_prev_page)
        sh *= 2
    sh = 1
    while sh < npg:
        x = x + jnp.where(row >= sh, pltpu.roll(x, sh, 0), 0.0)
        sh *= 2
    return x


def _fox_sample_kernel(pt_ref, q_ref, kn_ref, vn_ref, f_ref, bf_ref, lftab_ref, *rest, group):
    kc_refs, vc_refs = rest[:group], rest[group:2 * group]
    o_ref, lf_ref, q_s, m_s, l_s, acc_s, c_s = rest[2 * group:]
    b, p = pl.program_id(0), pl.program_id(1)
    npages = c_s.shape[0]
    one_head = slice(0, HEAD_DIM)

    @pl.when(p == 0)
    def _():
        q_s[...] = (_stack_heads(q_ref[...]) * (HEAD_DIM ** -0.5)).astype(BF16)
        m_s[...] = jnp.full_like(m_s, M_FLOOR)
        l_s[...] = jnp.zeros_like(l_s)
        acc_s[...] = jnp.zeros_like(acc_s)
        for pg in range(npages):
            c_s[pg:pg + 1, :] = lftab_ref[pl.ds(pt_ref[b, pg], 1), :]
        c_s[...] = _cumsum_keys(c_s[...])

    same_head = _same_head_past()
    q = q_s[...]
    s_tiles = [jnp.where(same_head,
                         _nt(q, kc_refs[g][0, 0].astype(BF16)) - c_s[pl.ds(p * group + g, 1), :],
                         NEG_INF) for g in range(group)]
    _online_update(s_tiles, [vc_refs[g][0, 0].astype(BF16) for g in range(group)],
                   m_s, l_s, acc_s, one_head)

    @pl.when(p == pl.num_programs(1) - 1)
    def _():
        lane = _iota((1, LANES), 1)
        lf = jax.nn.log_sigmoid(f_ref[0] + bf_ref[...])
        lf_ref[0] = lf
        cum = lf
        for sh in (1, 2, 4):
            cum = cum + jnp.where((lane & (SUBLANES - 1)) >= sh, pltpu.roll(cum, sh, 1), 0.0)
        tail = c_s[npages - 1:npages, PAGE_ROWS - LANES:]
        spread = ((_iota((LANES, LANES), 0) - (LANES - N_HEADS))
                  == (_iota((LANES, LANES), 1) // SUBLANES)).astype(F32)
        past = jnp.dot(jnp.broadcast_to(tail, (SUBLANES, LANES)), spread, precision=HI,
                       preferred_element_type=F32)[0:1]
        sn = _nt(q_s[...], _pad_rows(_stack_heads(kn_ref[...])).astype(BF16)) - (past + cum)
        sn = jnp.where(_visible_new(), sn, NEG_INF)
        _online_update(sn, _pad_rows(_stack_heads(vn_ref[...])).astype(BF16), m_s, l_s, acc_s, one_head)
        o_ref[...] = _unstack_heads(acc_s[...] / l_s[...])


PAGE_GROUP = 8


def _page_specs(slot, group):
    return [pl.BlockSpec((1, 1, PAGE_ROWS, HEAD_DIM),
                         lambda b, p, pt, g=g: (slot, pt[b, p * group + g], 0, 0))
            for g in range(group)]


def _fox_sample(page_table, q, kn, vn, fflat, bflat, lftab, kc, vc, slot):
    n = q.shape[0]
    nb, npages = page_table.shape
    group = PAGE_GROUP
    assert npages % group == 0

    def rows(w):
        return pl.BlockSpec((SUBLANES, w), lambda b, p, pt: (b, 0))

    pages = _page_specs(slot, group)
    flat = pl.BlockSpec((1, 1, LANES), lambda b, p, pt: (b, 0, 0))
    grid_spec = pltpu.PrefetchScalarGridSpec(
        num_scalar_prefetch=1,
        grid=(nb, npages // group),
        in_specs=[rows(D_MODEL), rows(D_MODEL), rows(D_MODEL), flat,
                  _resident((1, LANES)), _resident(lftab.shape)] + pages + pages,
        out_specs=[rows(D_MODEL), flat],
        scratch_shapes=[pltpu.VMEM((HT_ROWS, HEAD_DIM), BF16),
                        pltpu.VMEM((HT_ROWS, 1), F32), pltpu.VMEM((HT_ROWS, 1), F32),
                        pltpu.VMEM((HT_ROWS, HEAD_DIM), F32),
                        pltpu.VMEM((npages, PAGE_ROWS), F32)])
    return pl.pallas_call(
        functools.partial(_fox_sample_kernel, group=group),
        grid_spec=grid_spec,
        out_shape=[jax.ShapeDtypeStruct((n, D_MODEL), F32), jax.ShapeDtypeStruct((nb, 1, LANES), F32)],
        compiler_params=_params("parallel", "arbitrary"),
        name="fox_sample",
    )(page_table, q, kn, vn, fflat, bflat, lftab, *([kc] * group), *([vc] * group))


def _top_blocks(gate, n_valid):
    lane = _iota(gate.shape, 1)
    g = jnp.where(lane < n_valid, gate, NEG_INF)
    sel = jnp.zeros(gate.shape, F32)
    for _ in range(MOBA_TOPK):
        mx = jnp.max(g, axis=-1, keepdims=True)
        idx = jnp.min(jnp.where(g == mx, lane, LANES), axis=-1, keepdims=True)
        hit = lane == idx
        sel = jnp.where(hit & (mx > NEG_INF), 1.0, sel)
        g = jnp.where(hit, NEG_INF, g)
    return sel


def _top_blocks_t(gate_t, n_valid):
    nb = gate_t.shape[0]
    blk = _iota(gate_t.shape, 0)
    g = jnp.where(blk < n_valid, gate_t, NEG_INF)
    sel = jnp.zeros(gate_t.shape, F32)
    for _ in range(MOBA_TOPK):
        mx = jnp.max(g, axis=0, keepdims=True)
        idx = jnp.min(jnp.where(g == mx, blk, nb), axis=0, keepdims=True)
        hit = blk == idx
        sel = jnp.where(hit & (mx > NEG_INF), 1.0, sel)
        g = jnp.where(hit, NEG_INF, g)
    return sel


MOBA_TILE = 2 * MOBA_BLOCK


def _moba_prompt_kernel(q_ref, k_ref, v_ref, km_ref, o_ref, qs_s, sel_s, m_s, l_s, acc_s):
    i, j = pl.program_id(1), pl.program_id(2)
    tq = MOBA_TILE
    blocks_per_tile = tq // MOBA_BLOCK

    @pl.when(j == 0)
    def _():
        qs_s[...] = (q_ref[...] * (HEAD_DIM ** -0.5)).astype(BF16)
        m_s[...] = jnp.full_like(m_s, M_FLOOR)
        l_s[...] = jnp.zeros_like(l_s)
        acc_s[...] = jnp.zeros_like(acc_s)
        own = i * blocks_per_tile + _iota((1, tq), 1) // MOBA_BLOCK
        for h in range(N_HEADS):
            hs = slice(h * HEAD_DIM, (h + 1) * HEAD_DIM)
            sel_t = _top_blocks_t(_nt(km_ref[0, :, hs], q_ref[:, hs], HI), own)
            sel_s[h] = _pad_rows(sel_t).T.astype(BF16)

    def attend(diagonal):
        width = blocks_per_tile * LANES
        spread = (_iota((LANES, width), 0)
                  == j * blocks_per_tile + _iota((LANES, width), 1) // LANES).astype(BF16)
        if diagonal:
            row, col = _iota((tq, tq), 0), _iota((tq, tq), 1)
            own_causal = ((row // MOBA_BLOCK) == (col // MOBA_BLOCK)) & (col <= row)
        for h in range(N_HEADS):
            hs = slice(h * HEAD_DIM, (h + 1) * HEAD_DIM)
            picked = jnp.dot(sel_s[h], spread, preferred_element_type=F32)
            visible = jnp.concatenate(
                [picked[:, b * LANES:(b + 1) * LANES] for b in range(blocks_per_tile)
                 for _ in range(MOBA_BLOCK // LANES)], axis=1) > 0.5
            if diagonal:
                visible = visible | own_causal
            s = jnp.where(visible, _nt(qs_s[:, hs], k_ref[:, hs].astype(BF16)), NEG_INF)
            _flash_update(s, v_ref[:, hs], m_s.at[h], l_s.at[h], acc_s, hs)

    @pl.when(j < i)
    def _():
        attend(False)

    @pl.when(j == i)
    def _():
        attend(True)
        for h in range(N_HEADS):
            hs = slice(h * HEAD_DIM, (h + 1) * HEAD_DIM)
            o_ref[:, hs] = acc_s[:, hs] / l_s[h]


def _moba_prompt(q, k, v, kmean, nseq):
    n = q.shape[0]
    tq = MOBA_TILE
    nq = n // nseq // tq
    assert kmean.shape[1] % SUBLANES == 0 and kmean.shape[1] <= LANES
    qspec = pl.BlockSpec((tq, D_MODEL), lambda b, i, j: (b * nq + i, 0))
    kspec = pl.BlockSpec((tq, D_MODEL), lambda b, i, j: (b * nq + jnp.minimum(j, i), 0))
    return pl.pallas_call(
        _moba_prompt_kernel,
        grid=(nseq, nq, nq),
        in_specs=[qspec, kspec, kspec,
                  pl.BlockSpec((1, kmean.shape[1], D_MODEL), lambda b, i, j: (b, 0, 0))],
        out_specs=qspec,
        out_shape=jax.ShapeDtypeStruct((n, D_MODEL), F32),
        scratch_shapes=[pltpu.VMEM((tq, D_MODEL), BF16), pltpu.VMEM((N_HEADS, tq, LANES), BF16),
                        pltpu.VMEM((N_HEADS, tq, LANES), F32), pltpu.VMEM((N_HEADS, tq, LANES), F32),
                        pltpu.VMEM((tq, D_MODEL), F32)],
        compiler_params=_params("parallel", "parallel", "arbitrary"),
        name="moba_prompt",
    )(q, k, v, kmean)


def _moba_sample_kernel(pt_ref, q_ref, kn_ref, vn_ref, *rest, group, pages_per_block):
    kc_refs, vc_refs = rest[:group], rest[group:2 * group]
    o_ref, qf_s, qb_s, m_all, l_all, acc_all, ksum_s = rest[2 * group:]
    p = pl.program_id(1)
    n_blocks = acc_all.shape[0]
    blocks_per_step = group // pages_per_block

    @pl.when(p == 0)
    def _():
        q = _stack_heads(q_ref[...])
        qf_s[...] = q
        qb_s[...] = (q * (HEAD_DIM ** -0.5)).astype(BF16)
        ksum_s[...] = jnp.zeros_like(ksum_s)

    same_head = _same_head_past()
    q = qb_s[...]
    for bi in range(blocks_per_step):
        blk = p * blocks_per_step + bi
        ksum = jnp.zeros((N_HEADS, HEAD_DIM), F32)
        s_tiles, v_tiles = [], []
        for g in range(bi * pages_per_block, (bi + 1) * pages_per_block):
            kpage = kc_refs[g][0, 0]
            ksum = ksum + jnp.sum(kpage.reshape(LANES, N_HEADS, HEAD_DIM), axis=0)
            s_tiles.append(jnp.where(same_head, _nt(q, kpage.astype(BF16)), NEG_INF))
            v_tiles.append(vc_refs[g][0, 0].astype(BF16))
        ksum_s[pl.ds(pl.multiple_of(blk * N_HEADS, N_HEADS), N_HEADS), :] = ksum
        mp = s_tiles[0].max(axis=-1, keepdims=True)
        for st in s_tiles[1:]:
            mp = jnp.maximum(mp, jnp.max(st, axis=-1, keepdims=True))
        lsum = jnp.zeros((HT_ROWS, 1), F32)
        acc = jnp.zeros((HT_ROWS, HEAD_DIM), F32)
        for st, vt in zip(s_tiles, v_tiles):
            e = jnp.exp(st - mp)
            lsum = lsum + jnp.sum(e, axis=-1, keepdims=True)
            acc = acc + jnp.dot(e.astype(BF16), vt, preferred_element_type=F32)
        m_all[blk] = mp
        l_all[blk] = lsum
        acc_all[blk] = acc

    @pl.when(p == pl.num_programs(1) - 1)
    def _():
        kmean = ksum_s[...] * (1.0 / MOBA_BLOCK)
        g = _nt(qf_s[...], kmean, HI)
        same = (_iota((HT_ROWS, LANES), 1) & (N_HEADS - 1)) == (_iota((HT_ROWS, LANES), 0) // SUBLANES)
        group = ((_iota((LANES, LANES), 0) // N_HEADS) == _iota((LANES, LANES), 1)).astype(F32)
        gate = jnp.dot(jnp.where(same, g, 0.0), group, precision=HI, preferred_element_type=F32)
        sel = _top_blocks(gate, n_blocks)
        so = _nt(qb_s[...], _pad_rows(_stack_heads(kn_ref[...])).astype(BF16))
        so = jnp.where(_visible_new(), so, NEG_INF)
        m_own = jnp.max(so, axis=-1, keepdims=True)
        e_own = jnp.exp(so - m_own)
        l_own = jnp.sum(e_own, axis=-1, keepdims=True)
        acc_own = jnp.dot(e_own.astype(BF16), _pad_rows(_stack_heads(vn_ref[...])).astype(BF16),
                          preferred_element_type=F32)
        picks = [sel[:, n:n + 1] > 0.0 for n in range(n_blocks)]
        m_tot = m_own
        for n, pick in enumerate(picks):
            m_tot = jnp.maximum(m_tot, jnp.where(pick, m_all[n], NEG_INF))
        w_own = jnp.exp(m_own - m_tot)
        acc = w_own * acc_own
        l_tot = w_own * l_own
        for n, pick in enumerate(picks):
            w = jnp.where(pick, jnp.exp(m_all[n] - m_tot), 0.0)
            acc = acc + w * acc_all[n]
            l_tot = l_tot + w * l_all[n]
        o_ref[...] = _unstack_heads(acc / l_tot)


def _moba_sample(page_table, q, kn, vn, kc, vc, slot):
    n = q.shape[0]
    nb, npages = page_table.shape
    pages_per_block = MOBA_BLOCK // (kc.shape[2] // N_HEADS)
    n_blocks = npages // pages_per_block
    group = PAGE_GROUP
    assert n_blocks * N_HEADS <= LANES and group % pages_per_block == 0 and npages % group == 0

    def rows(w):
        return pl.BlockSpec((SUBLANES, w), lambda b, p, pt: (b, 0))

    pages = _page_specs(slot, group)
    grid_spec = pltpu.PrefetchScalarGridSpec(
        num_scalar_prefetch=1,
        grid=(nb, npages // group),
        in_specs=[rows(D_MODEL), rows(D_MODEL), rows(D_MODEL)] + pages + pages,
        out_specs=rows(D_MODEL),
        scratch_shapes=[pltpu.VMEM((HT_ROWS, HEAD_DIM), F32), pltpu.VMEM((HT_ROWS, HEAD_DIM), BF16),
                        pltpu.VMEM((n_blocks, HT_ROWS, 1), F32), pltpu.VMEM((n_blocks, HT_ROWS, 1), F32),
                        pltpu.VMEM((n_blocks, HT_ROWS, HEAD_DIM), F32),
                        pltpu.VMEM((LANES, HEAD_DIM), F32)])
    return pl.pallas_call(
        functools.partial(_moba_sample_kernel, group=group, pages_per_block=pages_per_block),
        grid_spec=grid_spec,
        out_shape=jax.ShapeDtypeStruct((n, D_MODEL), F32),
        compiler_params=_params("parallel", "arbitrary"),
        name="moba_sample",
    )(page_table, q, kn, vn, *([kc] * group), *([vc] * group))


def _pad_lanes(a, width=LANES):
    return jnp.pad(a, [(0, 0)] * (a.ndim - 1) + [(0, width - a.shape[-1])])


def _rope_tables(pos):
    half = HEAD_DIM // 2
    inv = ROPE_THETA ** (-jnp.arange(half, dtype=F32) / half)
    ang = pos.astype(F32)[:, None] * inv[None, :]
    cos, sin = jnp.cos(ang), jnp.sin(ang)
    return jnp.concatenate([cos, cos], axis=-1), jnp.concatenate([-sin, sin], axis=-1)


def kernel(x_prompt, x_sample, cache_fox_k, cache_fox_v, cache_fox_logf, cache_moba_k, cache_moba_v,
           state_mlstm_c, state_mlstm_n, state_mlstm_m, state_ffn_conv, page_table,
           norm_mix_g, norm_ffn_g, norm_final_g,
           mlstm_w_in, mlstm_b_gates, mlstm_norm_g, mlstm_w_out,
           fox_w_in, fox_b_f, fox_w_out, moba_w_in, moba_w_out,
           ffn_w_up, ffn_conv_w, ffn_conv_b, ffn_w_down):
    B, S, D = x_prompt.shape
    Bd, T, _ = x_sample.shape
    H, Dh, dk = N_HEADS, HEAD_DIM, MLSTM_QK
    depth = norm_mix_g.shape[0]
    n_pool, page_rows = cache_fox_k.shape[1], cache_fox_k.shape[2]
    past = page_table.shape[1] * page_rows
    assert T == SUBLANES and D == D_MODEL and page_rows == LANES

    xp = x_prompt.reshape(B * S, D)
    xs = x_sample.reshape(Bd * T, D)
    fkp, fks, fvp, fvs, flp, fls = [], [], [], [], [], []
    mkp, mks, mvp, mvs = [], [], [], []
    acp, acs, anp, ans, amp, ams = [], [], [], [], [], []
    cvp, cvs = [], []
    g_mix = norm_mix_g.reshape(depth, 1, D)
    ffn_params = (norm_ffn_g.reshape(depth, 1, D), ffn_w_up.astype(BF16), ffn_conv_w,
                  ffn_conv_b.reshape(depth, 1, 2 * D_FF), ffn_w_down.astype(BF16))
    ffn_prev = state_ffn_conv.astype(F32)

    for i in range(depth):
        kind, slot = i % N_MIXERS, i // N_MIXERS
        if kind == 0:
            w = mlstm_w_in[slot]
            wq = w[:, :H * dk] * (dk ** -0.5)
            wk = w[:, H * dk:2 * H * dk]
            rest = w[:, 2 * H * dk:2 * H * dk + 2 * H * Dh]
            wig = _pad_lanes(w[:, 2 * H * dk + 2 * H * Dh:2 * H * dk + 2 * H * Dh + H])
            wfg = _pad_lanes(w[:, 2 * H * dk + 2 * H * Dh + H:])
            pad_heads = lambda a: _pad_lanes(a.reshape(D, H, dk), Dh).reshape(D, H * Dh)
            w_p = jnp.concatenate([pad_heads(wq), pad_heads(wk), rest, wig, wfg], axis=1).astype(BF16)
            w_s = jnp.concatenate([wq, wk, rest, wig, wfg], axis=1).astype(BF16)
            bi = _pad_lanes(mlstm_b_gates[slot][:H].reshape(1, H))
            bfg = _pad_lanes(mlstm_b_gates[slot][H:].reshape(1, H))
            ng = mlstm_norm_g[slot].reshape(1, H * Dh)
            wo = mlstm_w_out[slot].astype(BF16)

            q, k, v, o, ig, fg = _norm_proj(xp, g_mix, i, w_p, [H * Dh, H * Dh, H * Dh, H * Dh, LANES, LANES])
            hp, c_, n_, m_ = _mlstm_prompt(q, k, v, o, ig, fg, bi, bfg, ng, B)
            acp.append(c_); anp.append(n_[:, :, :dk]); amp.append(m_.reshape(B, H))
            mix_p = hp

            q, k, v, o, ig, fg = _norm_proj(xs, g_mix, i, w_s, [H * dk, H * dk, H * Dh, H * Dh, LANES, LANES])
            hs_, c_, n_, m_ = _mlstm_sample(q, k, v, o, ig, fg, bi, bfg, ng,
                                            state_mlstm_c[slot], state_mlstm_n[slot], state_mlstm_m[slot])
            acs.append(c_); ans.append(n_.reshape(Bd, H, dk)); ams.append(m_.reshape(Bd, H))
            mix_s = hs_
        elif kind == 1:
            w = fox_w_in[slot]
            w_b = jnp.concatenate([w[:, :3 * H * Dh], _pad_lanes(w[:, 3 * H * Dh:])], axis=1).astype(BF16)
            bfg = _pad_lanes(fox_b_f[slot].reshape(1, H))
            wo = fox_w_out[slot].astype(BF16)
            widths = [H * Dh, H * Dh, H * Dh, LANES]

            q, k, v, fraw = _norm_proj(xp, g_mix, i, w_b, widths)
            lf, crow = _fox_gates(fraw, bfg, B)
            op = _fox_prompt(q, k, v, crow, B)
            fkp.append(k.reshape(B, S, H, Dh)); fvp.append(v.reshape(B, S, H, Dh))
            flp.append(lf[:, :H].reshape(B, S, H))
            mix_p = op

            q, k, v, fraw = _norm_proj(xs, g_mix, i, w_b, widths)
            fflat = _pad_lanes(fraw[:, :H].reshape(Bd, T, H).transpose(0, 2, 1).reshape(Bd, 1, H * T))
            bflat = _pad_lanes(jnp.repeat(fox_b_f[slot], T).reshape(1, H * T))
            n_slots = cache_fox_k.shape[0]
            os_, lf = _fox_sample(page_table, q, k, v, fflat, bflat,
                                  cache_fox_logf[slot].astype(F32).reshape(n_pool, page_rows * H),
                                  cache_fox_k.astype(F32).reshape(n_slots, n_pool, page_rows * H, Dh),
                                  cache_fox_v.astype(F32).reshape(n_slots, n_pool, page_rows * H, Dh), slot)
            fks.append(k.reshape(Bd, T, H, Dh)); fvs.append(v.reshape(Bd, T, H, Dh))
            fls.append(lf[:, 0, :H * T].reshape(Bd, H, T).transpose(0, 2, 1))
            mix_s = os_
        else:
            w_b = moba_w_in[slot].astype(BF16)
            wo = moba_w_out[slot].astype(BF16)
            cos_p, sin_p = _rope_tables(jnp.arange(S, dtype=jnp.int32))
            cos_s, sin_s = _rope_tables(past + jnp.arange(T, dtype=jnp.int32))
            reps = min(MOBA_BLOCK, Bd * T) // T
            cos_s, sin_s = jnp.tile(cos_s, (reps, 1)), jnp.tile(sin_s, (reps, 1))

            q, k, v, km = _norm_proj_rope(xp, g_mix, i, w_b, cos_p, sin_p)
            nblk = S // MOBA_BLOCK
            km = jnp.pad(km.reshape(B, nblk, D), ((0, 0), (0, -nblk % SUBLANES), (0, 0)))
            op = _moba_prompt(q, k, v, km, B)
            mkp.append(k.reshape(B, S, H, Dh)); mvp.append(v.reshape(B, S, H, Dh))
            mix_p = op

            q, k, v, _ = _norm_proj_rope(xs, g_mix, i, w_b, cos_s, sin_s)
            n_slots = cache_moba_k.shape[0]
            os_ = _moba_sample(page_table, q, k, v,
                               cache_moba_k.astype(F32).reshape(n_slots, n_pool, page_rows * H, Dh),
                               cache_moba_v.astype(F32).reshape(n_slots, n_pool, page_rows * H, Dh), slot)
            mks.append(k.reshape(Bd, T, H, Dh)); mvs.append(v.reshape(Bd, T, H, Dh))
            mix_s = os_

        final_g = norm_final_g.reshape(1, D) if i == depth - 1 else None
        xp, tail = _ffn_prompt(xp, mix_p, wo, ffn_params, i, B, final_g)
        cvp.append(tail.reshape(B, SUBLANES, 2 * D_FF)[:, SUBLANES - (CONV_W - 1):])
        xs, new = _ffn_sample(xs, mix_s, wo, ffn_params, i, ffn_prev, final_g)
        cvs.append(new)

    y_prompt = xp.reshape(B, S, D)
    y_sample = xs.reshape(Bd, T, D)
    return (y_prompt, y_sample,
            jnp.stack(fkp), jnp.stack(fks), jnp.stack(fvp), jnp.stack(fvs), jnp.stack(flp), jnp.stack(fls),
            jnp.stack(mkp), jnp.stack(mks), jnp.stack(mvp), jnp.stack(mvs),
            jnp.stack(acp), jnp.stack(acs), jnp.stack(anp), jnp.stack(ans), jnp.stack(amp), jnp.stack(ams),
            jnp.stack(cvp), jnp.stack(cvs))
```

```python
import functools

import jax
import jax.numpy as jnp
from jax import lax
from jax.experimental import pallas as pl
from jax.experimental.pallas import tpu as pltpu

F32 = jnp.float32
BF16 = jnp.bfloat16
HI = lax.Precision.HIGHEST

D_MODEL = 1024
N_HEADS = 8
HEAD_DIM = 128
MLSTM_QK = 64
D_FF = 2816
CONV_W = 3
N_MIXERS = 3
MOBA_BLOCK = 256
MOBA_TOPK = 3
ROPE_THETA = 10000.0
RMS_EPS = 1e-6
NEG_INF = float("-inf")
M_FLOOR = -1e30
LANES = 128
SUBLANES = 8
VMEM_LIMIT = 56 * 1024 * 1024

NT_DIMS = (((1,), (1,)), ((), ()))
TN_DIMS = (((0,), (0,)), ((), ()))


def _params(*sem):
    return pltpu.CompilerParams(dimension_semantics=sem, vmem_limit_bytes=VMEM_LIMIT)


def _resident(shape):
    nd = len(shape)
    return pl.BlockSpec(shape, lambda *_: (0,) * nd, pipeline_mode=pl.Buffered(1))


def _rms(x, g):
    return x * lax.rsqrt(jnp.mean(x * x, axis=-1, keepdims=True) + RMS_EPS) * g


def _iota(shape, axis):
    return lax.broadcasted_iota(jnp.int32, shape, axis)


def _eye_rows(n):
    return (_iota((n, LANES), 0) == _iota((n, LANES), 1)).astype(F32)


def _nt(a, b, precision=None):
    return lax.dot_general(a, b, NT_DIMS, precision=precision, preferred_element_type=F32)


def _norm_proj_kernel(x_ref, g_ref, w_ref, *o_refs, widths):
    xn = _rms(x_ref[...], g_ref[0]).astype(BF16)
    off = 0
    for o_ref, wd in zip(o_refs, widths):
        for c in range(0, wd, 512):
            cw = min(512, wd - c)
            o_ref[:, c:c + cw] = jnp.dot(xn, w_ref[:, off + c:off + c + cw],
                                         preferred_element_type=F32)
        off += wd


def _norm_proj(x, g, layer, w, widths, tm=256):
    n = x.shape[0]
    tm = min(tm, n)
    assert sum(widths) == w.shape[1] and n % tm == 0
    return pl.pallas_call(
        functools.partial(_norm_proj_kernel, widths=tuple(widths)),
        grid=(n // tm,),
        in_specs=[pl.BlockSpec((tm, D_MODEL), lambda i: (i, 0)),
                  _layer_block(g, layer),
                  _resident(w.shape)],
        out_specs=[pl.BlockSpec((tm, wd), lambda i: (i, 0)) for wd in widths],
        out_shape=[jax.ShapeDtypeStruct((n, wd), F32) for wd in widths],
        compiler_params=_params("parallel"),
        name="norm_proj",
    )(x, g, w)


def _norm_proj_rope_kernel(x_ref, g_ref, w_ref, cos_ref, sin_ref, q_ref, k_ref, v_ref, km_ref):
    xn = _rms(x_ref[...], g_ref[0]).astype(BF16)
    cos, sin = cos_ref[...], sin_ref[...]
    pair = 2 * HEAD_DIM
    for dst, off in ((q_ref, 0), (k_ref, D_MODEL)):
        for c in range(0, D_MODEL, pair):
            t = jnp.dot(xn, w_ref[:, off + c:off + c + pair], preferred_element_type=F32)
            for half in range(2):
                lo = c + half * HEAD_DIM
                xh = t[:, half * HEAD_DIM:(half + 1) * HEAD_DIM]
                rot = xh * cos + pltpu.roll(xh, HEAD_DIM // 2, 1) * sin
                dst[:, lo:lo + HEAD_DIM] = rot
                if dst is k_ref:
                    km_ref[0, :, lo:lo + HEAD_DIM] = jnp.mean(rot, axis=0, keepdims=True)
    for c in range(0, D_MODEL, 512):
        v_ref[:, c:c + 512] = jnp.dot(xn, w_ref[:, 2 * D_MODEL + c:2 * D_MODEL + c + 512],
                                      preferred_element_type=F32)


def _norm_proj_rope(x, g, layer, w, cos, sin, tm=MOBA_BLOCK):
    n = x.shape[0]
    tm = min(tm, n)
    ntab = cos.shape[0] // tm
    tok = pl.BlockSpec((tm, D_MODEL), lambda i: (i, 0))
    tab = pl.BlockSpec((tm, HEAD_DIM), lambda i: (i % ntab, 0))
    return pl.pallas_call(
        _norm_proj_rope_kernel,
        grid=(n // tm,),
        in_specs=[tok, _layer_block(g, layer), _resident(w.shape), tab, tab],
        out_specs=[tok, tok, tok, pl.BlockSpec((1, 1, D_MODEL), lambda i: (i, 0, 0))],
        out_shape=[jax.ShapeDtypeStruct((n, D_MODEL), F32)] * 3
                  + [jax.ShapeDtypeStruct((n // tm, 1, D_MODEL), F32)],
        compiler_params=_params("parallel"),
        name="norm_proj_rope",
    )(x, g, w, cos, sin)


FFN_CH = 256


def _layer_block(arr, layer):
    nd = arr.ndim
    return pl.BlockSpec((1,) + arr.shape[1:], lambda *_: (layer,) + (0,) * (nd - 1),
                        pipeline_mode=pl.Buffered(1))


def _ffn_kernel(*refs, tt, grouped, final):
    x_ref, mix_ref, wo_ref, g_ref, wup_ref, cw_ref, cb_ref, wdn_ref = refs[:8]
    refs = refs[8:]
    if final:
        gf_ref, refs = refs[0], refs[1:]
    if grouped:
        prev_ref, o_ref, new_ref, hbuf, p_s, u_s = refs
        groups = tt // SUBLANES
        p_s[...] = jnp.zeros_like(p_s)
    else:
        o_ref, tail_ref, hbuf, carry = refs

        @pl.when(pl.program_id(1) == 0)
        def _():
            carry[...] = jnp.zeros_like(carry)

    x = x_ref[...] + jnp.dot(mix_ref[...].astype(BF16), wo_ref[...], preferred_element_type=F32)
    xn = _rms(x, g_ref[0]).astype(BF16)
    row = _iota((tt, 1), 0)
    for c in range(D_FF // FFN_CH):
        ys = []
        for part in range(2):
            c0 = part * D_FF + c * FFN_CH
            cols = slice(c0, c0 + FFN_CH)
            u = jnp.dot(xn, wup_ref[0, :, cols], preferred_element_type=F32)
            r1 = pltpu.roll(u, 1, 0)
            r2 = pltpu.roll(u, 2, 0)
            if grouped:
                for k in range(FFN_CH // LANES):
                    lo = c0 + k * LANES
                    p_s[k, pl.ds(0, groups, stride=SUBLANES), :] = prev_ref[0, :, 0, lo:lo + LANES]
                    p_s[k, pl.ds(1, groups, stride=SUBLANES), :] = prev_ref[0, :, 1, lo:lo + LANES]
                    u_s[k] = u[:, k * LANES:(k + 1) * LANES]
                    new_ref[:, 0, lo:lo + LANES] = u_s[k, pl.ds(SUBLANES - 2, groups, stride=SUBLANES), :]
                    new_ref[:, 1, lo:lo + LANES] = u_s[k, pl.ds(SUBLANES - 1, groups, stride=SUBLANES), :]
                p2 = jnp.concatenate([p_s[k] for k in range(FFN_CH // LANES)], axis=1)
                p1 = pltpu.roll(p2, tt - 1, 0)
                t8 = row & (SUBLANES - 1)
                um1 = jnp.where(t8 < 1, p1, r1)
                um2 = jnp.where(t8 < 2, p2, r2)
            else:
                pc = carry[:, cols]
                row8 = row[:SUBLANES]
                f1 = jnp.where(row8 < 1, pltpu.roll(pc, 1, 0), r1[:SUBLANES])
                f2 = jnp.where(row8 < 2, pltpu.roll(pc, 2, 0), r2[:SUBLANES])
                um1 = jnp.concatenate([f1, r1[SUBLANES:]], axis=0)
                um2 = jnp.concatenate([f2, r2[SUBLANES:]], axis=0)
                tail = u[tt - SUBLANES:]
                carry[:, cols] = tail
                tail_ref[:, cols] = tail
            y = cb_ref[0, :, cols] + cw_ref[0, 0:1, cols] * um2
            y = y + cw_ref[0, 1:2, cols] * um1
            y = y + cw_ref[0, 2:3, cols] * u
            ys.append(y)
        gate, val = ys
        hbuf[:, c * FFN_CH:(c + 1) * FFN_CH] = (gate * jax.nn.sigmoid(gate) * val).astype(BF16)
    out = x + jnp.dot(hbuf[...], wdn_ref[0], preferred_element_type=F32)
    o_ref[...] = _rms(out, gf_ref[...]) if final else out


def _ffn_params(wo, g, wup, cw, cb, wdn, layer, final_g):
    specs = [_resident(wo.shape)] + [_layer_block(a, layer) for a in (g, wup, cw, cb, wdn)]
    return specs + ([_resident(final_g.shape)] if final_g is not None else [])


def _ffn_prompt(x, mix, wo, params, layer, nseq, final_g=None, tt=512):
    n = x.shape[0]
    nt = n // nseq // tt
    tok = pl.BlockSpec((tt, D_MODEL), lambda b, t: (b * nt + t, 0))
    extra = () if final_g is None else (final_g,)
    return pl.pallas_call(
        functools.partial(_ffn_kernel, tt=tt, grouped=False, final=final_g is not None),
        grid=(nseq, nt),
        in_specs=[tok, tok] + _ffn_params(wo, *params, layer, final_g),
        out_specs=[tok, pl.BlockSpec((SUBLANES, 2 * D_FF), lambda b, t: (b, 0))],
        out_shape=[jax.ShapeDtypeStruct((n, D_MODEL), F32),
                   jax.ShapeDtypeStruct((nseq * SUBLANES, 2 * D_FF), F32)],
        scratch_shapes=[pltpu.VMEM((tt, D_FF), BF16), pltpu.VMEM((SUBLANES, 2 * D_FF), F32)],
        compiler_params=_params("parallel", "arbitrary"),
        name="ffn_prompt",
    )(x, mix, wo, *params, *extra)


def _ffn_sample(x, mix, wo, params, layer, prev, final_g=None, tt=256):
    n = x.shape[0]
    tt = min(tt, n)
    groups = tt // SUBLANES
    hist = CONV_W - 1
    tok = pl.BlockSpec((tt, D_MODEL), lambda i: (i, 0))
    extra = () if final_g is None else (final_g,)
    return pl.pallas_call(
        functools.partial(_ffn_kernel, tt=tt, grouped=True, final=final_g is not None),
        grid=(n // tt,),
        in_specs=[tok, tok] + _ffn_params(wo, *params, layer, final_g)
                 + [pl.BlockSpec((1, groups, hist, 2 * D_FF), lambda i: (layer, i, 0, 0))],
        out_specs=[tok, pl.BlockSpec((groups, hist, 2 * D_FF), lambda i: (i, 0, 0))],
        out_shape=[jax.ShapeDtypeStruct((n, D_MODEL), F32),
                   jax.ShapeDtypeStruct((n // SUBLANES, hist, 2 * D_FF), F32)],
        scratch_shapes=[pltpu.VMEM((tt, D_FF), BF16), pltpu.VMEM((FFN_CH // LANES, tt, LANES), F32),
                        pltpu.VMEM((FFN_CH // LANES, tt, LANES), F32)],
        compiler_params=_params("parallel"),
        name="ffn_sample",
    )(x, mix, wo, *params, *extra, prev)


def _mlstm_prompt_kernel(q_ref, k_ref, v_ref, o_ref, ig_ref, fg_ref, bi_ref, bf_ref, ng_ref,
                         h_ref, cout_ref, nout_ref, mout_ref, c_s, m_s, *, chunk):
    L = chunk
    j = pl.program_id(1)

    def tile(x, n):
        return jnp.concatenate([x] * n, axis=1)

    @pl.when(j == 0)
    def _():
        c_s[...] = jnp.zeros_like(c_s)
        m_s[...] = jnp.zeros_like(m_s)

    ig = ig_ref[...] + bi_ref[...]
    lf = jax.nn.log_sigmoid(fg_ref[...] + bf_ref[...])
    tril = (_iota((L, L), 0) >= _iota((L, L), 1))
    bt = jnp.dot(tril.astype(F32), lf, precision=HI, preferred_element_type=F32)
    rc = ig - bt
    r_t = _nt(_eye_rows(SUBLANES), rc, HI)
    mean_mat = jnp.full((HEAD_DIM, HEAD_DIM), 1.0 / HEAD_DIM, F32)
    ones_b = jnp.ones((L, LANES), BF16)
    for h in range(N_HEADS):
        hs = slice(h * HEAD_DIM, (h + 1) * HEAD_DIM)
        kh = k_ref[:, hs]
        qb, kb = q_ref[:, hs].astype(BF16), kh.astype(BF16)
        v_ones = jnp.concatenate([v_ref[:, hs].astype(BF16), ones_b], axis=1)
        bcol = jnp.broadcast_to(bt[:, h:h + 1], (L, LANES))
        rcol = jnp.broadcast_to(rc[:, h:h + 1], (L, LANES))
        dm = jnp.where(tril, tile(bcol, L // LANES) + r_t[h:h + 1, :], NEG_INF)
        m_prev = m_s[h]
        inter = bcol + m_prev
        mt = jnp.maximum(jnp.max(dm, axis=-1, keepdims=True), inter)
        w_intra = jnp.exp(dm - tile(mt, L // LANES))
        w_inter = jnp.exp(inter - mt)
        c_prev = c_s[h]
        s = _nt(qb, kb) * w_intra
        tot = (jnp.dot(s.astype(BF16), v_ones, preferred_element_type=F32)
               + jnp.dot(qb, c_prev.astype(BF16), preferred_element_type=F32) * tile(w_inter, 2))
        den = jnp.maximum(jnp.abs(tot[:, HEAD_DIM:]), jnp.exp(-mt))
        hh = tot[:, :HEAD_DIM] / den
        ms = jnp.dot(hh * hh, mean_mat, precision=HI, preferred_element_type=F32)
        hh = hh * lax.rsqrt(ms + RMS_EPS) * ng_ref[:, hs]
        h_ref[:, hs] = hh * jax.nn.sigmoid(o_ref[:, hs])
        b_last = bcol[L - 1:L, :]
        gcol = b_last + rcol
        m_new = jnp.maximum(b_last + m_prev, jnp.max(gcol, axis=0, keepdims=True))
        a_prev = jnp.exp(b_last + m_prev - m_new)
        ka = kh * jnp.exp(gcol - m_new)
        c_s[h] = tile(a_prev, 2) * c_prev + lax.dot_general(ka.astype(BF16), v_ones, TN_DIMS,
                                                            preferred_element_type=F32)
        m_s[h] = m_new

    @pl.when(j == pl.num_programs(1) - 1)
    def _():
        for h in range(N_HEADS):
            cout_ref[0, h] = c_s[h][:MLSTM_QK, :HEAD_DIM]
            nout_ref[0, h:h + 1, :] = c_s[h][:, HEAD_DIM:].T[0:1, :]
            mout_ref[0, :, h:h + 1] = m_s[h][:, 0:1]


def _mlstm_prompt(q, k, v, o, ig, fg, bi, bfg, ng, nseq, chunk=256):
    n = q.shape[0]
    nc = n // nseq // chunk
    wide = pl.BlockSpec((chunk, D_MODEL), lambda b, j: (b * nc + j, 0))
    narrow = pl.BlockSpec((chunk, LANES), lambda b, j: (b * nc + j, 0))
    return pl.pallas_call(
        functools.partial(_mlstm_prompt_kernel, chunk=chunk),
        grid=(nseq, nc),
        in_specs=[wide, wide, wide, wide, narrow, narrow,
                  _resident((1, LANES)), _resident((1, LANES)), _resident((1, D_MODEL))],
        out_specs=[wide,
                   pl.BlockSpec((1, N_HEADS, MLSTM_QK, HEAD_DIM), lambda b, j: (b, 0, 0, 0)),
                   pl.BlockSpec((1, N_HEADS, LANES), lambda b, j: (b, 0, 0)),
                   pl.BlockSpec((1, 1, N_HEADS), lambda b, j: (b, 0, 0))],
        out_shape=[jax.ShapeDtypeStruct((n, D_MODEL), F32),
                   jax.ShapeDtypeStruct((nseq, N_HEADS, MLSTM_QK, HEAD_DIM), F32),
                   jax.ShapeDtypeStruct((nseq, N_HEADS, LANES), F32),
                   jax.ShapeDtypeStruct((nseq, 1, N_HEADS), F32)],
        scratch_shapes=[pltpu.VMEM((N_HEADS, HEAD_DIM, 2 * HEAD_DIM), F32),
                        pltpu.VMEM((N_HEADS, 1, LANES), F32)],
        compiler_params=_params("parallel", "arbitrary"),
        name="mlstm_prompt",
    )(q, k, v, o, ig, fg, bi, bfg, ng)


def _cumsum_rows8(x):
    row = _iota(x.shape, 0)
    for sh in (1, 2, 4):
        x = x + jnp.where(row >= sh, pltpu.roll(x, sh, 0), 0.0)
    return x


def _stack_cols(x, n=N_HEADS):
    return jnp.concatenate([x[:, h:h + 1] for h in range(n)], axis=0)


def _stack_bcast(x, rows, n=N_HEADS):
    return jnp.concatenate([jnp.broadcast_to(x[:, h:h + 1], (rows, 1)) for h in range(n)], axis=0)


def _pad_rows(x, rows=LANES):
    return jnp.concatenate([x, jnp.zeros((rows - x.shape[0], x.shape[1]), x.dtype)], axis=0)


def _mlstm_sample_kernel(q_ref, k_ref, v_ref, o_ref, ig_ref, fg_ref, bi_ref, bf_ref, ng_ref,
                         c0_ref, n0_ref, m0_ref, h_ref, cout_ref, nout_ref, mout_ref):
    T, HT, QW = SUBLANES, N_HEADS * SUBLANES, N_HEADS * MLSTM_QK
    ig = ig_ref[...] + bi_ref[...]
    lf = jax.nn.log_sigmoid(fg_ref[...] + bf_ref[...])
    bt = _cumsum_rows8(lf)
    rc = ig - bt
    m0 = m0_ref[0]
    bcol = _stack_cols(bt)
    r_t = _nt(_eye_rows(SUBLANES), _pad_rows(rc), HI)
    rrow = jnp.concatenate([jnp.broadcast_to(r_t[h:h + 1, :], (T, LANES))
                            for h in range(N_HEADS)], axis=0)
    lane = _iota((HT, LANES), 1)
    tok = _iota((HT, LANES), 0) & (T - 1)
    dm = jnp.where(lane <= tok, bcol + rrow, NEG_INF)
    inter = bcol + _stack_bcast(m0, T)
    mt = jnp.maximum(jnp.max(dm, axis=-1, keepdims=True), inter)
    w_intra = jnp.exp(dm - mt)
    w_inter = jnp.exp(inter - mt)

    head_of_row = _iota((HT, QW), 0) // T
    head_of_lane = _iota((HT, QW), 1) // MLSTM_QK
    diag = head_of_row == head_of_lane
    q8, k8, v8 = q_ref[...], k_ref[...], v_ref[...]
    qbd = jnp.where(diag, jnp.concatenate([q8] * N_HEADS, axis=0), 0.0)
    qbd_b = qbd.astype(BF16)
    s = _nt(qbd_b, _pad_rows(k8).astype(BF16)) * w_intra
    nv = jnp.dot(s.astype(BF16), _pad_rows(v8).astype(BF16), preferred_element_type=F32)
    row_head = _iota((HT, HEAD_DIM), 0) // T
    num = jnp.zeros((HT, HEAD_DIM), F32)
    for h in range(N_HEADS):
        num = num + jnp.where(row_head == h, nv[:, h * HEAD_DIM:(h + 1) * HEAD_DIM], 0.0)
    c_prev = c0_ref[0].reshape(QW, HEAD_DIM)
    n_prev = n0_ref[0]
    num = num + jnp.dot(qbd_b, c_prev.astype(BF16), preferred_element_type=F32) * w_inter
    den = (jnp.sum(s, axis=-1, keepdims=True)
           + jnp.sum(qbd * n_prev, axis=-1, keepdims=True) * w_inter)
    den = jnp.maximum(jnp.abs(den), jnp.exp(-mt))
    hh = num / den
    ng = jnp.concatenate([jnp.broadcast_to(ng_ref[:, h * HEAD_DIM:(h + 1) * HEAD_DIM], (T, HEAD_DIM))
                          for h in range(N_HEADS)], axis=0)
    o8 = o_ref[...]
    ost = jnp.concatenate([o8[:, h * HEAD_DIM:(h + 1) * HEAD_DIM] for h in range(N_HEADS)], axis=0)
    hh = hh * lax.rsqrt(jnp.mean(hh * hh, axis=-1, keepdims=True) + RMS_EPS) * ng
    hh = hh * jax.nn.sigmoid(ost)
    h_ref[...] = jnp.concatenate([hh[h * T:(h + 1) * T, :] for h in range(N_HEADS)], axis=1)

    b_last = bt[T - 1:T, :]
    g2 = b_last + rc
    m_new = jnp.maximum(b_last + m0, jnp.max(g2, axis=0, keepdims=True))
    a_prev = jnp.exp(b_last + m0 - m_new)
    a_tok = jnp.exp(g2 - m_new)
    kabd = jnp.where(diag, jnp.concatenate([k8] * N_HEADS, axis=0) * _stack_cols(a_tok), 0.0)
    vst = jnp.concatenate([v8[:, h * HEAD_DIM:(h + 1) * HEAD_DIM] for h in range(N_HEADS)], axis=0)
    dc = lax.dot_general(kabd.astype(BF16), vst.astype(BF16), TN_DIMS, preferred_element_type=F32)
    c_new = _stack_bcast(a_prev, MLSTM_QK) * c_prev + dc
    cout_ref[0] = c_new.reshape(N_HEADS, MLSTM_QK, HEAD_DIM)
    a_lane = jnp.max(jnp.where(diag, _stack_bcast(a_prev, T), 0.0), axis=0, keepdims=True)
    nout_ref[0] = a_lane * n_prev + jnp.sum(kabd, axis=0, keepdims=True)
    mout_ref[0] = m_new[:, :N_HEADS]


def _mlstm_sample(q, k, v, o, ig, fg, bi, bfg, ng, c0, n0, m0):
    n = q.shape[0]
    nb = n // SUBLANES
    qw = N_HEADS * MLSTM_QK

    def rows(w):
        return pl.BlockSpec((SUBLANES, w), lambda b: (b, 0))

    cspec = pl.BlockSpec((1, N_HEADS, MLSTM_QK, HEAD_DIM), lambda b: (b, 0, 0, 0))
    nspec = pl.BlockSpec((1, 1, qw), lambda b: (b, 0, 0))
    mspec = pl.BlockSpec((1, 1, N_HEADS), lambda b: (b, 0, 0))
    return pl.pallas_call(
        _mlstm_sample_kernel,
        grid=(nb,),
        in_specs=[rows(qw), rows(qw), rows(D_MODEL), rows(D_MODEL), rows(LANES), rows(LANES),
                  _resident((1, LANES)), _resident((1, LANES)), _resident((1, D_MODEL)),
                  cspec, nspec, pl.BlockSpec((1, 1, LANES), lambda b: (b, 0, 0))],
        out_specs=[rows(D_MODEL), cspec, nspec, mspec],
        out_shape=[jax.ShapeDtypeStruct((n, D_MODEL), F32),
                   jax.ShapeDtypeStruct((nb, N_HEADS, MLSTM_QK, HEAD_DIM), F32),
                   jax.ShapeDtypeStruct((nb, 1, qw), F32),
                   jax.ShapeDtypeStruct((nb, 1, N_HEADS), F32)],
        compiler_params=_params("parallel"),
        name="mlstm_sample",
    )(q, k, v, o, ig, fg, bi, bfg, ng, c0, n0.reshape(nb, 1, qw), _pad_lanes(m0).reshape(nb, 1, LANES))


def _fox_gate_kernel(f_ref, bf_ref, lf_ref, crow_ref, carry, *, ts):
    @pl.when(pl.program_id(1) == 0)
    def _():
        carry[...] = jnp.zeros_like(carry)

    lf = jax.nn.log_sigmoid(f_ref[...] + bf_ref[...])
    tril = (_iota((ts, ts), 0) >= _iota((ts, ts), 1)).astype(F32)
    c = jnp.dot(tril, lf, precision=HI, preferred_element_type=F32) + carry[...]
    carry[...] = c[ts - 1:ts, :]
    lf_ref[...] = lf
    crow_ref[0] = _nt(_eye_rows(SUBLANES), c, HI)


def _fox_gates(fraw, bfg, nseq, ts=512):
    n = fraw.shape[0]
    seq = n // nseq
    nt = seq // ts
    tok = pl.BlockSpec((ts, LANES), lambda b, t: (b * nt + t, 0))
    return pl.pallas_call(
        functools.partial(_fox_gate_kernel, ts=ts),
        grid=(nseq, nt),
        in_specs=[tok, _resident((1, LANES))],
        out_specs=[tok, pl.BlockSpec((1, SUBLANES, ts), lambda b, t: (b, 0, t))],
        out_shape=[jax.ShapeDtypeStruct((n, LANES), F32),
                   jax.ShapeDtypeStruct((nseq, SUBLANES, seq), F32)],
        scratch_shapes=[pltpu.VMEM((1, LANES), F32)],
        compiler_params=_params("parallel", "arbitrary"),
        name="fox_gates",
    )(fraw, bfg)


def _online_update(s, vb, m_ref, l_ref, acc_ref, hs):
    s_tiles, v_tiles = (s, vb) if isinstance(s, (list, tuple)) else ([s], [vb])
    m_prev = m_ref[...]
    m_new = m_prev
    for st in s_tiles:
        m_new = jnp.maximum(m_new, jnp.max(st, axis=-1, keepdims=True))
    alpha = jnp.exp(m_prev - m_new)
    l_new = alpha * l_ref[...]
    acc = alpha * acc_ref[:, hs]
    for st, vt in zip(s_tiles, v_tiles):
        p = jnp.exp(st - m_new)
        l_new = l_new + jnp.sum(p, axis=-1, keepdims=True)
        acc = acc + jnp.dot(p.astype(BF16), vt, preferred_element_type=F32)
    l_ref[...] = l_new
    acc_ref[:, hs] = acc
    m_ref[...] = m_new


def _flash_update(s, v, m_ref, l_ref, acc_ref, hs):
    m_prev = m_ref[...]
    m_new = jnp.maximum(m_prev, jnp.max(s, axis=-1, keepdims=True))
    alpha = jnp.exp(m_prev - m_new)
    p = jnp.exp(s - jnp.concatenate([m_new] * (s.shape[1] // LANES), axis=1))
    v_ones = jnp.concatenate([v.astype(BF16), jnp.ones((v.shape[0], LANES), BF16)], axis=1)
    pv = jnp.dot(p.astype(BF16), v_ones, preferred_element_type=F32)
    acc_ref[:, hs] = alpha * acc_ref[:, hs] + pv[:, :HEAD_DIM]
    l_ref[...] = alpha * l_ref[...] + pv[:, HEAD_DIM:]
    m_ref[...] = m_new


def _causal_pairs(nq):
    pairs = [(i, j) for i in range(nq) for j in range(i + 1)]
    return (jnp.asarray([p[0] for p in pairs], jnp.int32), jnp.asarray([p[1] for p in pairs], jnp.int32))


def _fox_prompt_kernel(qi_ref, kj_ref, q_ref, k_ref, v_ref, crow_ref, o_ref, qs_s, m_s, l_s, acc_s, *, tq):
    t = pl.program_id(1)
    i, j = qi_ref[t], kj_ref[t]

    @pl.when(j == 0)
    def _():
        qs_s[...] = (q_ref[...] * (HEAD_DIM ** -0.5)).astype(BF16)
        m_s[...] = jnp.full_like(m_s, M_FLOOR)
        l_s[...] = jnp.zeros_like(l_s)
        acc_s[...] = jnp.zeros_like(acc_s)

    def attend(diagonal):
        causal = _iota((tq, tq), 1) <= _iota((tq, tq), 0)
        for h in range(N_HEADS):
            hs = slice(h * HEAD_DIM, (h + 1) * HEAD_DIM)
            s = _nt(qs_s[:, hs], k_ref[:, hs].astype(BF16)) - crow_ref[0, h:h + 1, :]
            if diagonal:
                s = jnp.where(causal, s, NEG_INF)
            _flash_update(s, v_ref[:, hs], m_s.at[h], l_s.at[h], acc_s, hs)

    @pl.when(j < i)
    def _():
        attend(False)

    @pl.when(j == i)
    def _():
        attend(True)
        for h in range(N_HEADS):
            hs = slice(h * HEAD_DIM, (h + 1) * HEAD_DIM)
            o_ref[:, hs] = acc_s[:, hs] / l_s[h]


def _fox_prompt(q, k, v, crow, nseq, tq=512):
    n = q.shape[0]
    nq = n // nseq // tq
    qi, kj = _causal_pairs(nq)
    qspec = pl.BlockSpec((tq, D_MODEL), lambda b, t, qi, kj: (b * nq + qi[t], 0))
    kspec = pl.BlockSpec((tq, D_MODEL), lambda b, t, qi, kj: (b * nq + kj[t], 0))
    grid_spec = pltpu.PrefetchScalarGridSpec(
        num_scalar_prefetch=2,
        grid=(nseq, qi.shape[0]),
        in_specs=[qspec, kspec, kspec,
                  pl.BlockSpec((1, SUBLANES, tq), lambda b, t, qi, kj: (b, 0, kj[t]))],
        out_specs=qspec,
        scratch_shapes=[pltpu.VMEM((tq, D_MODEL), BF16),
                        pltpu.VMEM((N_HEADS, tq, LANES), F32), pltpu.VMEM((N_HEADS, tq, LANES), F32),
                        pltpu.VMEM((tq, D_MODEL), F32)])
    return pl.pallas_call(
        functools.partial(_fox_prompt_kernel, tq=tq),
        grid_spec=grid_spec,
        out_shape=jax.ShapeDtypeStruct((n, D_MODEL), F32),
        compiler_params=_params("parallel", "arbitrary"),
        name="fox_prompt",
    )(qi, kj, q, k, v, crow)


HT_ROWS = N_HEADS * SUBLANES
PAGE_ROWS = LANES * N_HEADS


def _stack_heads(x8):
    return jnp.concatenate([x8[:, h * HEAD_DIM:(h + 1) * HEAD_DIM] for h in range(N_HEADS)], axis=0)


def _unstack_heads(x):
    return jnp.concatenate([x[h * SUBLANES:(h + 1) * SUBLANES, :] for h in range(N_HEADS)], axis=1)


def _same_head_past():
    return ((_iota((HT_ROWS, PAGE_ROWS), 1) & (N_HEADS - 1))
            == (_iota((HT_ROWS, PAGE_ROWS), 0) // SUBLANES))


def _visible_new():
    row, lane = _iota((HT_ROWS, LANES), 0), _iota((HT_ROWS, LANES), 1)
    return (((lane // SUBLANES) == (row // SUBLANES))
            & ((lane & (SUBLANES - 1)) <= (row & (SUBLANES - 1))))


def _cumsum_keys(x):
    npg, w = x.shape
    lane, row = _iota(x.shape, 1), _iota(x.shape, 0)
    sh = N_HEADS
    while sh < w:
        r = pltpu.roll(x, sh, 1)
        from_prev_page = jnp.where(row >= 1, pltpu.roll(r, 1, 0), 0.0)
        x = x + jnp.where(lane >= sh, r, from_prev_page)
        sh *= 2
    sh = 1
    while sh < npg:
        x = x + jnp.where(row >= sh, pltpu.roll(x, sh, 0), 0.0)
        sh *= 2
    return x


def _fox_sample_kernel(pt_ref, q_ref, kn_ref, vn_ref, f_ref, bf_ref, lftab_ref, *rest, group):
    kc_refs, vc_refs = rest[:group], rest[group:2 * group]
    o_ref, lf_ref, q_s, m_s, l_s, acc_s, c_s = rest[2 * group:]
    b, p = pl.program_id(0), pl.program_id(1)
    npages = c_s.shape[0]
    one_head = slice(0, HEAD_DIM)

    @pl.when(p == 0)
    def _():
        q_s[...] = (_stack_heads(q_ref[...]) * (HEAD_DIM ** -0.5)).astype(BF16)
        m_s[...] = jnp.full_like(m_s, M_FLOOR)
        l_s[...] = jnp.zeros_like(l_s)
        acc_s[...] = jnp.zeros_like(acc_s)
        for pg in range(npages):
            c_s[pg:pg + 1, :] = lftab_ref[pl.ds(pt_ref[b, pg], 1), :]
        c_s[...] = _cumsum_keys(c_s[...])

    same_head = _same_head_past()
    q = q_s[...]
    s_tiles = [jnp.where(same_head,
                         _nt(q, kc_refs[g][0, 0].astype(BF16)) - c_s[pl.ds(p * group + g, 1), :],
                         NEG_INF) for g in range(group)]
    _online_update(s_tiles, [vc_refs[g][0, 0].astype(BF16) for g in range(group)],
                   m_s, l_s, acc_s, one_head)

    @pl.when(p == pl.num_programs(1) - 1)
    def _():
        lane = _iota((1, LANES), 1)
        lf = jax.nn.log_sigmoid(f_ref[0] + bf_ref[...])
        lf_ref[0] = lf
        cum = lf
        for sh in (1, 2, 4):
            cum = cum + jnp.where((lane & (SUBLANES - 1)) >= sh, pltpu.roll(cum, sh, 1), 0.0)
        tail = c_s[npages - 1:npages, PAGE_ROWS - LANES:]
        spread = ((_iota((LANES, LANES), 0) - (LANES - N_HEADS))
                  == (_iota((LANES, LANES), 1) // SUBLANES)).astype(F32)
        past = jnp.dot(jnp.broadcast_to(tail, (SUBLANES, LANES)), spread, precision=HI,
                       preferred_element_type=F32)[0:1]
        sn = _nt(q_s[...], _pad_rows(_stack_heads(kn_ref[...])).astype(BF16)) - (past + cum)
        sn = jnp.where(_visible_new(), sn, NEG_INF)
        _online_update(sn, _pad_rows(_stack_heads(vn_ref[...])).astype(BF16), m_s, l_s, acc_s, one_head)
        o_ref[...] = _unstack_heads(acc_s[...] / l_s[...])


PAGE_GROUP = 8


def _page_specs(slot, group):
    return [pl.BlockSpec((1, 1, PAGE_ROWS, HEAD_DIM),
                         lambda b, p, pt, g=g: (slot, pt[b, p * group + g], 0, 0))
            for g in range(group)]


def _fox_sample(page_table, q, kn, vn, fflat, bflat, lftab, kc, vc, slot):
    n = q.shape[0]
    nb, npages = page_table.shape
    group = PAGE_GROUP
    assert npages % group == 0

    def rows(w):
        return pl.BlockSpec((SUBLANES, w), lambda b, p, pt: (b, 0))

    pages = _page_specs(slot, group)
    flat = pl.BlockSpec((1, 1, LANES), lambda b, p, pt: (b, 0, 0))
    grid_spec = pltpu.PrefetchScalarGridSpec(
        num_scalar_prefetch=1,
        grid=(nb, npages // group),
        in_specs=[rows(D_MODEL), rows(D_MODEL), rows(D_MODEL), flat,
                  _resident((1, LANES)), _resident(lftab.shape)] + pages + pages,
        out_specs=[rows(D_MODEL), flat],
        scratch_shapes=[pltpu.VMEM((HT_ROWS, HEAD_DIM), BF16),
                        pltpu.VMEM((HT_ROWS, 1), F32), pltpu.VMEM((HT_ROWS, 1), F32),
                        pltpu.VMEM((HT_ROWS, HEAD_DIM), F32),
                        pltpu.VMEM((npages, PAGE_ROWS), F32)])
    return pl.pallas_call(
        functools.partial(_fox_sample_kernel, group=group),
        grid_spec=grid_spec,
        out_shape=[jax.ShapeDtypeStruct((n, D_MODEL), F32), jax.ShapeDtypeStruct((nb, 1, LANES), F32)],
        compiler_params=_params("parallel", "arbitrary"),
        name="fox_sample",
    )(page_table, q, kn, vn, fflat, bflat, lftab, *([kc] * group), *([vc] * group))


def _top_blocks(gate, n_valid):
    lane = _iota(gate.shape, 1)
    g = jnp.where(lane < n_valid, gate, NEG_INF)
    sel = jnp.zeros(gate.shape, F32)
    for _ in range(MOBA_TOPK):
        mx = jnp.max(g, axis=-1, keepdims=True)
        idx = jnp.min(jnp.where(g == mx, lane, LANES), axis=-1, keepdims=True)
        hit = lane == idx
        sel = jnp.where(hit & (mx > NEG_INF), 1.0, sel)
        g = jnp.where(hit, NEG_INF, g)
    return sel


def _top_blocks_t(gate_t, n_valid):
    nb = gate_t.shape[0]
    blk = _iota(gate_t.shape, 0)
    g = jnp.where(blk < n_valid, gate_t, NEG_INF)
    sel = jnp.zeros(gate_t.shape, F32)
    for _ in range(MOBA_TOPK):
        mx = jnp.max(g, axis=0, keepdims=True)
        idx = jnp.min(jnp.where(g == mx, blk, nb), axis=0, keepdims=True)
        hit = blk == idx
        sel = jnp.where(hit & (mx > NEG_INF), 1.0, sel)
        g = jnp.where(hit, NEG_INF, g)
    return sel


MOBA_TILE = 2 * MOBA_BLOCK


def _moba_prompt_kernel(qi_ref, kj_ref, q_ref, k_ref, v_ref, km_ref, o_ref, qs_s, sel_s, m_s, l_s, acc_s):
    t = pl.program_id(1)
    i, j = qi_ref[t], kj_ref[t]
    tq = MOBA_TILE
    blocks_per_tile = tq // MOBA_BLOCK

    @pl.when(j == 0)
    def _():
        qs_s[...] = (q_ref[...] * (HEAD_DIM ** -0.5)).astype(BF16)
        m_s[...] = jnp.full_like(m_s, M_FLOOR)
        l_s[...] = jnp.zeros_like(l_s)
        acc_s[...] = jnp.zeros_like(acc_s)
        own = i * blocks_per_tile + _iota((1, tq), 1) // MOBA_BLOCK
        for h in range(N_HEADS):
            hs = slice(h * HEAD_DIM, (h + 1) * HEAD_DIM)
            sel_t = _top_blocks_t(_nt(km_ref[0, :, hs], q_ref[:, hs], HI), own)
            sel_s[h] = _pad_rows(sel_t).T.astype(BF16)

    def attend(diagonal):
        width = blocks_per_tile * LANES
        spread = (_iota((LANES, width), 0)
                  == j * blocks_per_tile + _iota((LANES, width), 1) // LANES).astype(BF16)
        if diagonal:
            row, col = _iota((tq, tq), 0), _iota((tq, tq), 1)
            own_causal = ((row // MOBA_BLOCK) == (col // MOBA_BLOCK)) & (col <= row)
        for h in range(N_HEADS):
            hs = slice(h * HEAD_DIM, (h + 1) * HEAD_DIM)
            picked = jnp.dot(sel_s[h], spread, preferred_element_type=F32)
            visible = jnp.concatenate(
                [picked[:, b * LANES:(b + 1) * LANES] for b in range(blocks_per_tile)
                 for _ in range(MOBA_BLOCK // LANES)], axis=1) > 0.5
            if diagonal:
                visible = visible | own_causal
            s = jnp.where(visible, _nt(qs_s[:, hs], k_ref[:, hs].astype(BF16)), NEG_INF)
            _flash_update(s, v_ref[:, hs], m_s.at[h], l_s.at[h], acc_s, hs)

    @pl.when(j < i)
    def _():
        attend(False)

    @pl.when(j == i)
    def _():
        attend(True)
        for h in range(N_HEADS):
            hs = slice(h * HEAD_DIM, (h + 1) * HEAD_DIM)
            o_ref[:, hs] = acc_s[:, hs] / l_s[h]


def _moba_prompt(q, k, v, kmean, nseq):
    n = q.shape[0]
    tq = MOBA_TILE
    nq = n // nseq // tq
    assert kmean.shape[1] % SUBLANES == 0 and kmean.shape[1] <= LANES
    qi, kj = _causal_pairs(nq)
    qspec = pl.BlockSpec((tq, D_MODEL), lambda b, t, qi, kj: (b * nq + qi[t], 0))
    kspec = pl.BlockSpec((tq, D_MODEL), lambda b, t, qi, kj: (b * nq + kj[t], 0))
    grid_spec = pltpu.PrefetchScalarGridSpec(
        num_scalar_prefetch=2,
        grid=(nseq, qi.shape[0]),
        in_specs=[qspec, kspec, kspec,
                  pl.BlockSpec((1, kmean.shape[1], D_MODEL), lambda b, t, qi, kj: (b, 0, 0))],
        out_specs=qspec,
        scratch_shapes=[pltpu.VMEM((tq, D_MODEL), BF16), pltpu.VMEM((N_HEADS, tq, LANES), BF16),
                        pltpu.VMEM((N_HEADS, tq, LANES), F32), pltpu.VMEM((N_HEADS, tq, LANES), F32),
                        pltpu.VMEM((tq, D_MODEL), F32)])
    return pl.pallas_call(
        _moba_prompt_kernel,
        grid_spec=grid_spec,
        out_shape=jax.ShapeDtypeStruct((n, D_MODEL), F32),
        compiler_params=_params("parallel", "arbitrary"),
        name="moba_prompt",
    )(qi, kj, q, k, v, kmean)


def _moba_sample_kernel(pt_ref, q_ref, kn_ref, vn_ref, *rest, group, pages_per_block):
    kc_refs, vc_refs = rest[:group], rest[group:2 * group]
    o_ref, qf_s, qb_s, m_all, l_all, acc_all, ksum_s = rest[2 * group:]
    p = pl.program_id(1)
    n_blocks = acc_all.shape[0]
    blocks_per_step = group // pages_per_block

    @pl.when(p == 0)
    def _():
        q = _stack_heads(q_ref[...])
        qf_s[...] = q
        qb_s[...] = (q * (HEAD_DIM ** -0.5)).astype(BF16)
        ksum_s[...] = jnp.zeros_like(ksum_s)

    same_head = _same_head_past()
    q = qb_s[...]
    for bi in range(blocks_per_step):
        blk = p * blocks_per_step + bi
        ksum = jnp.zeros((N_HEADS, HEAD_DIM), F32)
        s_tiles, v_tiles = [], []
        for g in range(bi * pages_per_block, (bi + 1) * pages_per_block):
            kpage = kc_refs[g][0, 0]
            ksum = ksum + jnp.sum(kpage.reshape(LANES, N_HEADS, HEAD_DIM), axis=0)
            s_tiles.append(jnp.where(same_head, _nt(q, kpage.astype(BF16)), NEG_INF))
            v_tiles.append(vc_refs[g][0, 0].astype(BF16))
        ksum_s[pl.ds(pl.multiple_of(blk * N_HEADS, N_HEADS), N_HEADS), :] = ksum
        mp = s_tiles[0].max(axis=-1, keepdims=True)
        for st in s_tiles[1:]:
            mp = jnp.maximum(mp, jnp.max(st, axis=-1, keepdims=True))
        lsum = jnp.zeros((HT_ROWS, 1), F32)
        acc = jnp.zeros((HT_ROWS, HEAD_DIM), F32)
        for st, vt in zip(s_tiles, v_tiles):
            e = jnp.exp(st - mp)
            lsum = lsum + jnp.sum(e, axis=-1, keepdims=True)
            acc = acc + jnp.dot(e.astype(BF16), vt, preferred_element_type=F32)
        m_all[blk] = mp
        l_all[blk] = lsum
        acc_all[blk] = acc

    @pl.when(p == pl.num_programs(1) - 1)
    def _():
        kmean = ksum_s[...] * (1.0 / MOBA_BLOCK)
        g = _nt(qf_s[...], kmean, HI)
        same = (_iota((HT_ROWS, LANES), 1) & (N_HEADS - 1)) == (_iota((HT_ROWS, LANES), 0) // SUBLANES)
        group = ((_iota((LANES, LANES), 0) // N_HEADS) == _iota((LANES, LANES), 1)).astype(F32)
        gate = jnp.dot(jnp.where(same, g, 0.0), group, precision=HI, preferred_element_type=F32)
        sel = _top_blocks(gate, n_blocks)
        so = _nt(qb_s[...], _pad_rows(_stack_heads(kn_ref[...])).astype(BF16))
        so = jnp.where(_visible_new(), so, NEG_INF)
        m_own = jnp.max(so, axis=-1, keepdims=True)
        e_own = jnp.exp(so - m_own)
        l_own = jnp.sum(e_own, axis=-1, keepdims=True)
        acc_own = jnp.dot(e_own.astype(BF16), _pad_rows(_stack_heads(vn_ref[...])).astype(BF16),
                          preferred_element_type=F32)
        picks = [sel[:, n:n + 1] > 0.0 for n in range(n_blocks)]
        m_tot = m_own
        for n, pick in enumerate(picks):
            m_tot = jnp.maximum(m_tot, jnp.where(pick, m_all[n], NEG_INF))
        w_own = jnp.exp(m_own - m_tot)
        acc = w_own * acc_own
        l_tot = w_own * l_own
        for n, pick in enumerate(picks):
            w = jnp.where(pick, jnp.exp(m_all[n] - m_tot), 0.0)
            acc = acc + w * acc_all[n]
            l_tot = l_tot + w * l_all[n]
        o_ref[...] = _unstack_heads(acc / l_tot)


def _moba_sample(page_table, q, kn, vn, kc, vc, slot):
    n = q.shape[0]
    nb, npages = page_table.shape
    pages_per_block = MOBA_BLOCK // (kc.shape[2] // N_HEADS)
    n_blocks = npages // pages_per_block
    group = PAGE_GROUP
    assert n_blocks * N_HEADS <= LANES and group % pages_per_block == 0 and npages % group == 0

    def rows(w):
        return pl.BlockSpec((SUBLANES, w), lambda b, p, pt: (b, 0))

    pages = _page_specs(slot, group)
    grid_spec = pltpu.PrefetchScalarGridSpec(
        num_scalar_prefetch=1,
        grid=(nb, npages // group),
        in_specs=[rows(D_MODEL), rows(D_MODEL), rows(D_MODEL)] + pages + pages,
        out_specs=rows(D_MODEL),
        scratch_shapes=[pltpu.VMEM((HT_ROWS, HEAD_DIM), F32), pltpu.VMEM((HT_ROWS, HEAD_DIM), BF16),
                        pltpu.VMEM((n_blocks, HT_ROWS, 1), F32), pltpu.VMEM((n_blocks, HT_ROWS, 1), F32),
                        pltpu.VMEM((n_blocks, HT_ROWS, HEAD_DIM), F32),
                        pltpu.VMEM((LANES, HEAD_DIM), F32)])
    return pl.pallas_call(
        functools.partial(_moba_sample_kernel, group=group, pages_per_block=pages_per_block),
        grid_spec=grid_spec,
        out_shape=jax.ShapeDtypeStruct((n, D_MODEL), F32),
        compiler_params=_params("parallel", "arbitrary"),
        name="moba_sample",
    )(page_table, q, kn, vn, *([kc] * group), *([vc] * group))


def _pad_lanes(a, width=LANES):
    return jnp.pad(a, [(0, 0)] * (a.ndim - 1) + [(0, width - a.shape[-1])])


def _rope_tables(pos):
    half = HEAD_DIM // 2
    inv = ROPE_THETA ** (-jnp.arange(half, dtype=F32) / half)
    ang = pos.astype(F32)[:, None] * inv[None, :]
    cos, sin = jnp.cos(ang), jnp.sin(ang)
    return jnp.concatenate([cos, cos], axis=-1), jnp.concatenate([-sin, sin], axis=-1)


def kernel(x_prompt, x_sample, cache_fox_k, cache_fox_v, cache_fox_logf, cache_moba_k, cache_moba_v,
           state_mlstm_c, state_mlstm_n, state_mlstm_m, state_ffn_conv, page_table,
           norm_mix_g, norm_ffn_g, norm_final_g,
           mlstm_w_in, mlstm_b_gates, mlstm_norm_g, mlstm_w_out,
           fox_w_in, fox_b_f, fox_w_out, moba_w_in, moba_w_out,
           ffn_w_up, ffn_conv_w, ffn_conv_b, ffn_w_down):
    B, S, D = x_prompt.shape
    Bd, T, _ = x_sample.shape
    H, Dh, dk = N_HEADS, HEAD_DIM, MLSTM_QK
    depth = norm_mix_g.shape[0]
    n_pool, page_rows = cache_fox_k.shape[1], cache_fox_k.shape[2]
    past = page_table.shape[1] * page_rows
    assert T == SUBLANES and D == D_MODEL and page_rows == LANES

    xp = x_prompt.reshape(B * S, D)
    xs = x_sample.reshape(Bd * T, D)
    fkp, fks, fvp, fvs, flp, fls = [], [], [], [], [], []
    mkp, mks, mvp, mvs = [], [], [], []
    acp, acs, anp, ans, amp, ams = [], [], [], [], [], []
    cvp, cvs = [], []
    g_mix = norm_mix_g.reshape(depth, 1, D)
    ffn_params = (norm_ffn_g.reshape(depth, 1, D), ffn_w_up.astype(BF16), ffn_conv_w,
                  ffn_conv_b.reshape(depth, 1, 2 * D_FF), ffn_w_down.astype(BF16))
    ffn_prev = state_ffn_conv.astype(F32)

    for i in range(depth):
        kind, slot = i % N_MIXERS, i // N_MIXERS
        if kind == 0:
            w = mlstm_w_in[slot]
            wq = w[:, :H * dk] * (dk ** -0.5)
            wk = w[:, H * dk:2 * H * dk]
            rest = w[:, 2 * H * dk:2 * H * dk + 2 * H * Dh]
            wig = _pad_lanes(w[:, 2 * H * dk + 2 * H * Dh:2 * H * dk + 2 * H * Dh + H])
            wfg = _pad_lanes(w[:, 2 * H * dk + 2 * H * Dh + H:])
            pad_heads = lambda a: _pad_lanes(a.reshape(D, H, dk), Dh).reshape(D, H * Dh)
            w_p = jnp.concatenate([pad_heads(wq), pad_heads(wk), rest, wig, wfg], axis=1).astype(BF16)
            w_s = jnp.concatenate([wq, wk, rest, wig, wfg], axis=1).astype(BF16)
            bi = _pad_lanes(mlstm_b_gates[slot][:H].reshape(1, H))
            bfg = _pad_lanes(mlstm_b_gates[slot][H:].reshape(1, H))
            ng = mlstm_norm_g[slot].reshape(1, H * Dh)
            wo = mlstm_w_out[slot].astype(BF16)

            q, k, v, o, ig, fg = _norm_proj(xp, g_mix, i, w_p, [H * Dh, H * Dh, H * Dh, H * Dh, LANES, LANES])
            hp, c_, n_, m_ = _mlstm_prompt(q, k, v, o, ig, fg, bi, bfg, ng, B)
            acp.append(c_); anp.append(n_[:, :, :dk]); amp.append(m_.reshape(B, H))
            mix_p = hp

            q, k, v, o, ig, fg = _norm_proj(xs, g_mix, i, w_s, [H * dk, H * dk, H * Dh, H * Dh, LANES, LANES])
            hs_, c_, n_, m_ = _mlstm_sample(q, k, v, o, ig, fg, bi, bfg, ng,
                                            state_mlstm_c[slot], state_mlstm_n[slot], state_mlstm_m[slot])
            acs.append(c_); ans.append(n_.reshape(Bd, H, dk)); ams.append(m_.reshape(Bd, H))
            mix_s = hs_
        elif kind == 1:
            w = fox_w_in[slot]
            w_b = jnp.concatenate([w[:, :3 * H * Dh], _pad_lanes(w[:, 3 * H * Dh:])], axis=1).astype(BF16)
            bfg = _pad_lanes(fox_b_f[slot].reshape(1, H))
            wo = fox_w_out[slot].astype(BF16)
            widths = [H * Dh, H * Dh, H * Dh, LANES]

            q, k, v, fraw = _norm_proj(xp, g_mix, i, w_b, widths)
            lf, crow = _fox_gates(fraw, bfg, B)
            op = _fox_prompt(q, k, v, crow, B)
            fkp.append(k.reshape(B, S, H, Dh)); fvp.append(v.reshape(B, S, H, Dh))
            flp.append(lf[:, :H].reshape(B, S, H))
            mix_p = op

            q, k, v, fraw = _norm_proj(xs, g_mix, i, w_b, widths)
            fflat = _pad_lanes(fraw[:, :H].reshape(Bd, T, H).transpose(0, 2, 1).reshape(Bd, 1, H * T))
            bflat = _pad_lanes(jnp.repeat(fox_b_f[slot], T).reshape(1, H * T))
            n_slots = cache_fox_k.shape[0]
            os_, lf = _fox_sample(page_table, q, k, v, fflat, bflat,
                                  cache_fox_logf[slot].astype(F32).reshape(n_pool, page_rows * H),
                                  cache_fox_k.astype(F32).reshape(n_slots, n_pool, page_rows * H, Dh),
                                  cache_fox_v.astype(F32).reshape(n_slots, n_pool, page_rows * H, Dh), slot)
            fks.append(k.reshape(Bd, T, H, Dh)); fvs.append(v.reshape(Bd, T, H, Dh))
            fls.append(lf[:, 0, :H * T].reshape(Bd, H, T).transpose(0, 2, 1))
            mix_s = os_
        else:
            w_b = moba_w_in[slot].astype(BF16)
            wo = moba_w_out[slot].astype(BF16)
            cos_p, sin_p = _rope_tables(jnp.arange(S, dtype=jnp.int32))
            cos_s, sin_s = _rope_tables(past + jnp.arange(T, dtype=jnp.int32))
            reps = min(MOBA_BLOCK, Bd * T) // T
            cos_s, sin_s = jnp.tile(cos_s, (reps, 1)), jnp.tile(sin_s, (reps, 1))

            q, k, v, km = _norm_proj_rope(xp, g_mix, i, w_b, cos_p, sin_p)
            nblk = S // MOBA_BLOCK
            km = jnp.pad(km.reshape(B, nblk, D), ((0, 0), (0, -nblk % SUBLANES), (0, 0)))
            op = _moba_prompt(q, k, v, km, B)
            mkp.append(k.reshape(B, S, H, Dh)); mvp.append(v.reshape(B, S, H, Dh))
            mix_p = op

            q, k, v, _ = _norm_proj_rope(xs, g_mix, i, w_b, cos_s, sin_s)
            n_slots = cache_moba_k.shape[0]
            os_ = _moba_sample(page_table, q, k, v,
                               cache_moba_k.astype(F32).reshape(n_slots, n_pool, page_rows * H, Dh),
                               cache_moba_v.astype(F32).reshape(n_slots, n_pool, page_rows * H, Dh), slot)
            mks.append(k.reshape(Bd, T, H, Dh)); mvs.append(v.reshape(Bd, T, H, Dh))
            mix_s = os_

        final_g = norm_final_g.reshape(1, D) if i == depth - 1 else None
        xp, tail = _ffn_prompt(xp, mix_p, wo, ffn_params, i, B, final_g)
        cvp.append(tail.reshape(B, SUBLANES, 2 * D_FF)[:, SUBLANES - (CONV_W - 1):])
        xs, new = _ffn_sample(xs, mix_s, wo, ffn_params, i, ffn_prev, final_g)
        cvs.append(new)

    y_prompt = xp.reshape(B, S, D)
    y_sample = xs.reshape(Bd, T, D)
    return (y_prompt, y_sample,
            jnp.stack(fkp), jnp.stack(fks), jnp.stack(fvp), jnp.stack(fvs), jnp.stack(flp), jnp.stack(fls),
            jnp.stack(mkp), jnp.stack(mks), jnp.stack(mvp), jnp.stack(mvs),
            jnp.stack(acp), jnp.stack(acs), jnp.stack(anp), jnp.stack(ans), jnp.stack(amp), jnp.stack(ams),
            jnp.stack(cvp), jnp.stack(cvs))
```
